```python
import math
import jax, jax.numpy as jnp
from jax import lax
import numpy as np

D_MODEL = 1024
BATCH = 4
SEQ = 4096
DEPTH = 2

D_INNER = 2 * D_MODEL
GROUP_WIDTH = D_INNER // 4
GRID_W = 64

ATTN_HEADS = 8
ATTN_KV_HEADS = 2
ATTN_HEAD_DIM = GROUP_WIDTH // ATTN_HEADS
Q_BLOCK = 128
ROPE_THETA = 10000.0

HGRN_HEADS = 4
HGRN_WIDTH = GROUP_WIDTH
HGRN_HEAD_V = HGRN_WIDTH // HGRN_HEADS
HGRN_EXPAND = 128
HGRN_KEY_WIDTH = HGRN_HEADS * HGRN_EXPAND

SSD_HEADS = 8
SSD_HEAD_DIM = GROUP_WIDTH // SSD_HEADS
SSD_WIDTH = SSD_HEADS * SSD_HEAD_DIM
SSD_GROUPS = 2
SSD_STATE = 128
SSD_CONV_WIDTH = 5
SSD_CONV_CH = SSD_WIDTH + 2 * SSD_GROUPS * SSD_STATE
SSD_CHUNK = 128

GLA_HEADS = 4
GLA_WIDTH = GROUP_WIDTH
GLA_KEY_WIDTH = GLA_WIDTH // 2
GLA_HEAD_K = GLA_KEY_WIDTH // GLA_HEADS
GLA_HEAD_V = GLA_WIDTH // GLA_HEADS
GLA_GATE_RANK = 16
GLA_GATE_NORMALIZER = 16.0

LIN_CHUNK = 64

DEEPNORM_ALPHA = (2 * DEPTH) ** 0.25
DEEPNORM_BETA = (8 * DEPTH) ** -0.25

IN_SPLITS = (
    ATTN_HEADS * ATTN_HEAD_DIM, ATTN_KV_HEADS * ATTN_HEAD_DIM, ATTN_KV_HEADS * ATTN_HEAD_DIM, GROUP_WIDTH,
    HGRN_KEY_WIDTH, HGRN_KEY_WIDTH, HGRN_KEY_WIDTH, HGRN_WIDTH, HGRN_WIDTH,
    SSD_CONV_CH, SSD_HEADS, SSD_HEADS, SSD_WIDTH,
    GLA_KEY_WIDTH, GLA_KEY_WIDTH, GLA_WIDTH, GLA_GATE_RANK, GLA_GATE_RANK, GLA_WIDTH,
)
N_IN = sum(IN_SPLITS)

kernel_name = 'hybrid_parallel_bidir_encoder'


def rms_norm(x, gain, eps=1e-6):
    xf = x.astype(jnp.float32)
    y = xf * lax.rsqrt(jnp.mean(xf * xf, axis=-1, keepdims=True) + eps)
    return (y * gain.astype(jnp.float32)).astype(x.dtype)


def layer_norm(x, gain, bias, eps=1e-5):
    xf = x.astype(jnp.float32)
    mu = jnp.mean(xf, axis=-1, keepdims=True)
    xc = xf - mu
    var = jnp.mean(xc * xc, axis=-1, keepdims=True)
    return (xc * lax.rsqrt(var + eps) * gain.astype(jnp.float32) + bias.astype(jnp.float32)).astype(x.dtype)


def split_columns(h, sizes):
    cuts = []
    acc = 0
    for w in sizes[:-1]:
        acc += w
        cuts.append(acc)
    return jnp.split(h, cuts, axis=-1)


def flip_seq(t):
    return jnp.flip(t, axis=1)


def axial_rope_tables(n_tokens):
    rows = n_tokens // GRID_W
    row_pos = jnp.repeat(jnp.arange(rows, dtype=jnp.float32), GRID_W)
    col_pos = jnp.tile(jnp.arange(GRID_W, dtype=jnp.float32), rows)
    axis_dim = ATTN_HEAD_DIM // 2
    inv_freq = jnp.power(ROPE_THETA, -jnp.arange(0, axis_dim, 2, dtype=jnp.float32) / axis_dim)
    ang_r = row_pos[:, None] * inv_freq
    ang_c = col_pos[:, None] * inv_freq
    return (jnp.cos(ang_r), jnp.sin(ang_r), jnp.cos(ang_c), jnp.sin(ang_c))


def _rotate(x, cos, sin):
    x1, x2 = jnp.split(x, 2, axis=-1)
    cos = cos[None, :, None, :].astype(x.dtype)
    sin = sin[None, :, None, :].astype(x.dtype)
    return jnp.concatenate([x1 * cos - x2 * sin, x2 * cos + x1 * sin], axis=-1)


def apply_axial_rope(x, rope):
    cos_r, sin_r, cos_c, sin_c = rope
    x_row, x_col = jnp.split(x, 2, axis=-1)
    return jnp.concatenate([_rotate(x_row, cos_r, sin_r), _rotate(x_col, cos_c, sin_c)], axis=-1)


def attention_branch(q_raw, k_raw, v_raw, z, q_gain, k_gain, rope):
    Bsz, L, _ = q_raw.shape
    group = ATTN_HEADS // ATTN_KV_HEADS
    q = rms_norm(q_raw.reshape(Bsz, L, ATTN_HEADS, ATTN_HEAD_DIM), q_gain)
    k = rms_norm(k_raw.reshape(Bsz, L, ATTN_KV_HEADS, ATTN_HEAD_DIM), k_gain)
    v = v_raw.reshape(Bsz, L, ATTN_KV_HEADS, ATTN_HEAD_DIM)
    q = apply_axial_rope(q, rope)
    k = apply_axial_rope(k, rope)
    q = q.reshape(Bsz, L // Q_BLOCK, Q_BLOCK, ATTN_KV_HEADS, group, ATTN_HEAD_DIM).transpose(1, 0, 2, 3, 4, 5)
    scale = ATTN_HEAD_DIM ** -0.5

    def attend(q_blk):
        s = jnp.einsum('bqkgd,bskd->bkgqs', q_blk, k).astype(jnp.float32) * scale
        p = jax.nn.softmax(s, axis=-1).astype(v.dtype)
        return jnp.einsum('bkgqs,bskd->bqkgd', p, v)

    o = lax.map(attend, q)
    o = o.transpose(1, 0, 2, 3, 4, 5).reshape(Bsz, L, ATTN_HEADS * ATTN_HEAD_DIM)
    return o * jax.nn.silu(z)


def chunked_gated_scan(q, k, v, log_g):
    Bsz, L, H, K = q.shape
    V = v.shape[-1]
    n_chunks = L // LIN_CHUNK
    f32 = jnp.float32

    def to_chunks(t):
        return t.astype(f32).reshape(Bsz, n_chunks, LIN_CHUNK, H, t.shape[-1]).transpose(1, 0, 3, 2, 4)

    qc, kc, vc, gc = to_chunks(q), to_chunks(k), to_chunks(v), to_chunks(log_g)
    lower = jnp.tril(jnp.ones((LIN_CHUNK, LIN_CHUNK), bool))[:, :, None]

    def step(state, blk):
        qi, ki, vi, gi = blk
        b = jnp.cumsum(gi, axis=2)
        rel = jnp.where(lower, b[:, :, :, None, :] - b[:, :, None, :, :], -jnp.inf)
        scores = jnp.einsum('bhtk,bhsk,bhtsk->bhts', qi, ki, jnp.exp(rel))
        out = (jnp.einsum('bhts,bhsv->bhtv', scores, vi)
               + jnp.einsum('bhtk,bhkv->bhtv', qi * jnp.exp(b), state))
        b_end = b[:, :, -1:, :]
        state = (jnp.exp(b_end[:, :, 0, :])[..., None] * state
                 + jnp.einsum('bhsk,bhsv->bhkv', ki * jnp.exp(b_end - b), vi))
        return state, out

    state0 = jnp.zeros((Bsz, H, K, V), f32)
    _, out = lax.scan(step, state0, (qc, kc, vc, gc))
    return out.transpose(1, 0, 3, 2, 4).reshape(Bsz, L, H, V).astype(v.dtype)


def bidirectional_gated_scan(q, k_fwd, log_g_fwd, k_bwd, log_g_bwd, v):
    o_f = chunked_gated_scan(q, k_fwd, v, log_g_fwd)
    o_b = flip_seq(chunked_gated_scan(flip_seq(q), flip_seq(k_bwd), flip_seq(v), flip_seq(log_g_bwd)))
    return o_f + o_b


def hgrn2_branch(q_raw, f_fwd_raw, f_bwd_raw, i_raw, z, lower_bound, norm_gain):
    Bsz, L, _ = q_raw.shape
    key_shape = (Bsz, L, HGRN_HEADS, HGRN_EXPAND)
    q = jax.nn.silu(q_raw).reshape(key_shape) * (HGRN_EXPAND ** -0.5)
    v = i_raw.reshape(Bsz, L, HGRN_HEADS, HGRN_HEAD_V)
    lb = jnp.maximum(lower_bound.astype(jnp.float32), 0.0).reshape(HGRN_HEADS, HGRN_EXPAND)
    log_lb = jnp.log(lb)
    log_1m_lb = jnp.log1p(-lb)

    def forget(f_raw):
        f = f_raw.astype(jnp.float32).reshape(key_shape)
        log_f = jnp.logaddexp(log_lb, log_1m_lb + jax.nn.log_sigmoid(f))
        one_minus_f = jnp.exp(log_1m_lb + jax.nn.log_sigmoid(-f))
        return one_minus_f, log_f

    k_f, g_f = forget(f_fwd_raw)
    k_b, g_b = forget(f_bwd_raw)
    o = bidirectional_gated_scan(q, k_f, g_f, k_b, g_b, v).reshape(Bsz, L, HGRN_WIDTH)
    return rms_norm(o, norm_gain) * jax.nn.silu(z)


def segsum(a):
    T = a.shape[-1]
    ae = jnp.broadcast_to(a[..., None], a.shape + (T,))
    cs = jnp.cumsum(jnp.where(jnp.tril(jnp.ones((T, T), bool), -1), ae, 0.0), axis=-2)
    return jnp.where(jnp.tril(jnp.ones((T, T), bool)), cs, -jnp.inf)


def ssd_scan(x, dt, a_coef, b_in, c_in):
    Bsz, L, H, P = x.shape
    G, N = b_in.shape[-2], b_in.shape[-1]
    R = H // G
    C = SSD_CHUNK
    nc = L // C
    f32 = jnp.float32
    xdt = (x.astype(f32) * dt[..., None]).reshape(Bsz, nc, C, G, R, P)
    a = (dt * a_coef).reshape(Bsz, nc, C, H).transpose(0, 1, 3, 2)
    a_cs = jnp.cumsum(a, axis=-1)
    bc = b_in.astype(f32).reshape(Bsz, nc, C, G, N)
    cc = c_in.astype(f32).reshape(Bsz, nc, C, G, N)
    decay_in = jnp.exp(segsum(a)).reshape(Bsz, nc, G, R, C, C)
    cb = jnp.einsum('bclgn,bcsgn->bcgls', cc, bc)
    y_diag = jnp.einsum('bcgls,bcgrls,bcsgrp->bclgrp', cb, decay_in, xdt)
    decay_to_end = jnp.exp(a_cs[..., -1:] - a_cs).reshape(Bsz, nc, G, R, C)
    states = jnp.einsum('bclgn,bcgrl,bclgrp->bcgrpn', bc, decay_to_end, xdt)
    states = jnp.concatenate([jnp.zeros_like(states[:, :1]), states], axis=1)
    chunk_a = jnp.pad(a_cs[..., -1].transpose(0, 2, 1), ((0, 0), (0, 0), (1, 0)))
    decay_chunk = jnp.exp(segsum(chunk_a)).reshape(Bsz, G, R, nc + 1, nc + 1)
    states = jnp.einsum('bgrzc,bcgrpn->bzgrpn', decay_chunk, states)[:, :-1]
    y_off = jnp.einsum('bclgn,bcgrpn,bcgrl->bclgrp', cc, states, jnp.exp(a_cs).reshape(Bsz, nc, G, R, C))
    return (y_diag + y_off).reshape(Bsz, L, H, P)


def centred_depthwise_conv(u, w, b):
    width, ch = w.shape
    y = lax.conv_general_dilated(u, w[:, None, :].astype(u.dtype), window_strides=(1,),
                                 padding=[((width - 1) // 2, width // 2)],
                                 dimension_numbers=('NWC', 'WIO', 'NWC'),
                                 feature_group_count=ch)
    return y + b.astype(u.dtype)


def ssd_branch(xbc, dt_fwd_raw, dt_bwd_raw, z, conv_w, conv_b, dt_bias, a_log, d_skip, norm_gain):
    Bsz, L, _ = xbc.shape
    f32 = jnp.float32
    u = jax.nn.silu(centred_depthwise_conv(xbc, conv_w, conv_b))
    xs, b_in, c_in = jnp.split(u, [SSD_WIDTH, SSD_WIDTH + SSD_GROUPS * SSD_STATE], axis=-1)
    xs = xs.reshape(Bsz, L, SSD_HEADS, SSD_HEAD_DIM)
    b_in = b_in.reshape(Bsz, L, SSD_GROUPS, SSD_STATE)
    c_in = c_in.reshape(Bsz, L, SSD_GROUPS, SSD_STATE)
    dt_f = jax.nn.softplus(dt_fwd_raw.astype(f32) + dt_bias[0].astype(f32))
    dt_b = jax.nn.softplus(dt_bwd_raw.astype(f32) + dt_bias[1].astype(f32))
    a_f = -jnp.exp(a_log[0].astype(f32))
    a_b = -jnp.exp(a_log[1].astype(f32))
    y_f = ssd_scan(xs, dt_f, a_f, b_in, c_in)
    y_b = flip_seq(ssd_scan(flip_seq(xs), flip_seq(dt_b), a_b, flip_seq(b_in), flip_seq(c_in)))
    y = y_f + y_b + d_skip.astype(f32)[:, None] * xs.astype(f32)
    y = y.reshape(Bsz, L, SSD_WIDTH).astype(z.dtype)
    return rms_norm(y * jax.nn.silu(z), norm_gain)


def gla_branch(q_raw, k_raw, v_raw, gk_fwd_low, gk_bwd_low, z, gk_w2, gk_b, norm_gain):
    Bsz, L, _ = q_raw.shape
    key_shape = (Bsz, L, GLA_HEADS, GLA_HEAD_K)
    q = q_raw.reshape(key_shape) * (GLA_HEAD_K ** -0.5)
    k = k_raw.reshape(key_shape)
    v = v_raw.reshape(Bsz, L, GLA_HEADS, GLA_HEAD_V)

    def log_gate(low, w2, b2):
        gk = jnp.einsum('bsr,rk->bsk', low, w2) + b2
        return (jax.nn.log_sigmoid(gk.astype(jnp.float32)) / GLA_GATE_NORMALIZER).reshape(key_shape)

    g_f = log_gate(gk_fwd_low, gk_w2[0], gk_b[0])
    g_b = log_gate(gk_bwd_low, gk_w2[1], gk_b[1])
    o = bidirectional_gated_scan(q, k, g_f, k, g_b, v)
    o = rms_norm(o, norm_gain).reshape(Bsz, L, GLA_WIDTH)
    return o * jax.nn.silu(z)


def setup_inputs(seed: int = 0) -> dict:
    key = jax.random.key(seed)
    ks = jax.random.split(key, 20)
    f32 = jnp.float32

    def nrm(k, shape, scale):
        return jax.random.normal(k, shape, f32) * scale

    x = nrm(ks[0], (BATCH, SEQ, D_MODEL), 1.0)
    w_in = nrm(ks[1], (DEPTH, D_MODEL, N_IN), D_MODEL ** -0.5)
    attn_q_norm = 1.0 + nrm(ks[2], (DEPTH, ATTN_HEAD_DIM), 0.02)
    attn_k_norm = 1.0 + nrm(ks[3], (DEPTH, ATTN_HEAD_DIM), 0.02)
    hgrn_lb_logits = nrm(ks[4], (DEPTH, HGRN_KEY_WIDTH), 0.1)
    hgrn_norm = 1.0 + nrm(ks[5], (DEPTH, HGRN_WIDTH), 0.02)
    ssd_conv_w = nrm(ks[6], (DEPTH, SSD_CONV_WIDTH, SSD_CONV_CH), SSD_CONV_WIDTH ** -0.5)
    ssd_conv_b = nrm(ks[7], (DEPTH, SSD_CONV_CH), 0.02)
    dt0 = jnp.exp(jax.random.uniform(ks[8], (DEPTH, 2, SSD_HEADS), f32,
                                     minval=math.log(1e-3), maxval=math.log(1e-1)))
    ssd_dt_bias = dt0 + jnp.log(-jnp.expm1(-dt0))
    ssd_a_log = jnp.log(jax.random.uniform(ks[9], (DEPTH, 2, SSD_HEADS), f32, minval=1.0, maxval=16.0))
    ssd_d = 1.0 + nrm(ks[10], (DEPTH, SSD_HEADS), 0.02)
    ssd_norm = 1.0 + nrm(ks[11], (DEPTH, SSD_WIDTH), 0.02)
    gla_gk_w2 = nrm(ks[12], (DEPTH, 2, GLA_GATE_RANK, GLA_KEY_WIDTH), GLA_GATE_RANK ** -0.5)
    gla_gk_b = nrm(ks[13], (DEPTH, 2, GLA_KEY_WIDTH), 0.02)
    gla_norm = 1.0 + nrm(ks[14], (DEPTH, GLA_HEAD_V), 0.02)
    w_out = nrm(ks[15], (DEPTH, D_INNER, D_MODEL), (D_INNER ** -0.5) * DEEPNORM_BETA)
    ln_g = 1.0 + nrm(ks[16], (DEPTH, D_MODEL), 0.02)
    ln_b = nrm(ks[17], (DEPTH, D_MODEL), 0.02)
    return {'x': x, 'w_in': w_in, 'attn_q_norm': attn_q_norm, 'attn_k_norm': attn_k_norm,
            'hgrn_lb_logits': hgrn_lb_logits, 'hgrn_norm': hgrn_norm,
            'ssd_conv_w': ssd_conv_w, 'ssd_conv_b': ssd_conv_b, 'ssd_dt_bias': ssd_dt_bias,
            'ssd_a_log': ssd_a_log, 'ssd_d': ssd_d, 'ssd_norm': ssd_norm,
            'gla_gk_w2': gla_gk_w2, 'gla_gk_b': gla_gk_b, 'gla_norm': gla_norm,
            'w_out': w_out, 'ln_g': ln_g, 'ln_b': ln_b}


def reference(x, w_in, attn_q_norm, attn_k_norm, hgrn_lb_logits, hgrn_norm,
              ssd_conv_w, ssd_conv_b, ssd_dt_bias, ssd_a_log, ssd_d, ssd_norm,
              gla_gk_w2, gla_gk_b, gla_norm, w_out, ln_g, ln_b):
    L = x.shape[1]
    rope = axial_rope_tables(L)
    lower_bounds = jnp.cumsum(jax.nn.softmax(hgrn_lb_logits.astype(jnp.float32), axis=0), axis=0)
    lower_bounds = lower_bounds - lower_bounds[0]
    for i in range(DEPTH):
        h = jnp.einsum('bsd,dn->bsn', x, w_in[i])
        (a_q, a_k, a_v, a_z,
         h_q, h_ff, h_fb, h_i, h_z,
         s_xbc, s_dtf, s_dtb, s_z,
         g_q, g_k, g_v, g_lf, g_lb, g_z) = split_columns(h, IN_SPLITS)
        y_a = attention_branch(a_q, a_k, a_v, a_z, attn_q_norm[i], attn_k_norm[i], rope)
        y_h = hgrn2_branch(h_q, h_ff, h_fb, h_i, h_z, lower_bounds[i], hgrn_norm[i])
        y_s = ssd_branch(s_xbc, s_dtf, s_dtb, s_z, ssd_conv_w[i], ssd_conv_b[i],
                         ssd_dt_bias[i], ssd_a_log[i], ssd_d[i], ssd_norm[i])
        y_g = gla_branch(g_q, g_k, g_v, g_lf, g_lb, g_z, gla_gk_w2[i], gla_gk_b[i], gla_norm[i])
        mixed = jnp.concatenate([y_a.astype(x.dtype), y_h.astype(x.dtype),
                                 y_s.astype(x.dtype), y_g.astype(x.dtype)], axis=-1)
        out = jnp.einsum('bsn,nd->bsd', mixed, w_out[i])
        x = layer_norm(DEEPNORM_ALPHA * x + out, ln_g[i], ln_b[i])
    return x
```

```python
import functools
import math

import numpy as np
import jax
import jax.numpy as jnp
from jax import lax
from jax.experimental import pallas as pl
from jax.experimental.pallas import tpu as pltpu

F32 = jnp.float32
BF16 = jnp.bfloat16

D_MODEL = 1024
DEPTH = 2
GROUP_WIDTH = 512
GRID_W = 64
ATTN_HEADS = 8
ATTN_KV_HEADS = 2
ATTN_HEAD_DIM = 64
ROPE_THETA = 10000.0
HGRN_HEADS = 4
HGRN_EXPAND = 128
SSD_HEADS = 8
SSD_HEAD_DIM = 64
SSD_GROUPS = 2
SSD_STATE = 128
SSD_CONV_WIDTH = 5
GLA_HEADS = 4
GLA_HEAD_K = 64
GLA_HEAD_V = 128
GLA_GATE_RANK = 16
GLA_GATE_NORMALIZER = 16.0
DEEPNORM_ALPHA = (2 * DEPTH) ** 0.25

LANES = 128
SUBLANES = 8
GW = GROUP_WIDTH

(G_AQ, G_AZ, G_AKV, G_HQ, G_HFF, G_HFB, G_HI, G_HZ,
 G_SX, G_SBC, G_SZ, G_GQ, G_GK, G_GV, G_GZ, G_MISC) = range(16)
N_GROUPS = 16
N_PROJ = N_GROUPS * GW

LIN_CHUNK = 64
LIN_SUB = 16
SSD_CHUNK = 128
SCAN_TOKENS = 512
ATTN_TQ = 256
VMEM_LIMIT = 48 * 1024 * 1024


def _dot(a, b):
    return jnp.dot(a, b, preferred_element_type=F32)


def _dot_nt(a, b):
    return lax.dot_general(a, b, (((1,), (1,)), ((), ())), preferred_element_type=F32)


def _dot_tn(a, b):
    return lax.dot_general(a, b, (((0,), (0,)), ((), ())), preferred_element_type=F32)


def _split3(x):
    hi = x.astype(BF16)
    r1 = x - hi.astype(F32)
    mid = r1.astype(BF16)
    lo = (r1 - mid.astype(F32)).astype(BF16)
    return hi, mid, lo


def _sel_dot(sel, x):
    hi, mid, lo = _split3(x)
    return _dot(sel, hi) + _dot(sel, mid) + _dot(sel, lo)


def _dot_sel(x, sel):
    hi, mid, lo = _split3(x)
    return _dot(hi, sel) + _dot(mid, sel) + _dot(lo, sel)


def _log_sigmoid(x):
    return jnp.minimum(x, 0.0) - jnp.log1p(jnp.exp(-jnp.abs(x)))


def _silu(x):
    return x * jax.nn.sigmoid(x)


def _softplus(x):
    return jnp.maximum(x, 0.0) + jnp.log1p(jnp.exp(-jnp.abs(x)))


def _params(sem):
    return pltpu.CompilerParams(dimension_semantics=sem, vmem_limit_bytes=VMEM_LIMIT)


def _matmul_kernel(x_ref, w_ref, o_ref):
    o_ref[...] = _dot(x_ref[...], w_ref[...])


def _in_proj(xb, w):
    t, d = xb.shape
    n = w.shape[1]
    tm = min(1024, t)
    tn = GW
    return pl.pallas_call(
        _matmul_kernel,
        grid=(t // tm, n // tn),
        in_specs=[pl.BlockSpec((tm, d), lambda i, j: (i, 0)),
                  pl.BlockSpec((d, tn), lambda i, j: (0, j))],
        out_specs=pl.BlockSpec((tm, tn), lambda i, j: (i, j)),
        out_shape=jax.ShapeDtypeStruct((t, n), F32),
        compiler_params=_params(("parallel", "parallel")),
        name="in_proj",
    )(xb, w)


def _head_rms(x, blocksum, gain, eps=1e-6):
    xx = x * x
    hi = xx.astype(BF16)
    lo = (xx - hi.astype(F32)).astype(BF16)
    ssq = _dot(hi, blocksum) + _dot(lo, blocksum)
    return x * lax.rsqrt(ssq * (1.0 / ATTN_HEAD_DIM) + eps) * gain


def _rope(x, cos, sin_signed):
    w = x.shape[1]
    lane = lax.broadcasted_iota(jnp.int32, x.shape, 1)
    first = (lane & 31) < 16
    partner = jnp.where(first, pltpu.roll(x, w - 16, 1), pltpu.roll(x, 16, 1))
    return x * cos + partner * sin_signed


def _attn_kernel(q_ref, z_ref, kv_ref, cos_ref, sin_ref, cosk_ref, sink_ref, qg_ref, kg_ref,
                 bs_ref, o_ref, kd_ref, vd_ref, qs_ref, *, seq, tq, prep_rows):
    n = pl.program_id(1)
    j = pl.program_id(2)

    @pl.when(jnp.logical_and(n == 0, j == 0))
    def _prep_kv():
        def body(i, carry):
            r0 = pl.multiple_of(i * prep_rows, prep_rows)
            kv = kv_ref[pl.ds(r0, prep_rows), :]
            k = kv[:, :LANES]
            v = kv[:, LANES:]
            k = _head_rms(k, bs_ref[0:LANES, 0:LANES], kg_ref[...])
            k = _rope(k, cosk_ref[pl.ds(r0, prep_rows), :], sink_ref[pl.ds(r0, prep_rows), :])
            lane = lax.broadcasted_iota(jnp.int32, k.shape, 1)
            low = lane < ATTN_HEAD_DIM
            k_sw = pltpu.roll(k, ATTN_HEAD_DIM, 1)
            v_sw = pltpu.roll(v, ATTN_HEAD_DIM, 1)
            kd_ref[0, pl.ds(r0, prep_rows), :] = jnp.where(low, k, k_sw).astype(BF16)
            kd_ref[1, pl.ds(r0, prep_rows), :] = jnp.where(low, k_sw, k).astype(BF16)
            vd_ref[0, pl.ds(r0, prep_rows), :] = jnp.where(low, v, v_sw).astype(BF16)
            vd_ref[1, pl.ds(r0, prep_rows), :] = jnp.where(low, v_sw, v).astype(BF16)
            return carry
        lax.fori_loop(0, seq // prep_rows, body, 0)

    @pl.when(j == 0)
    def _prep_q():
        q = _head_rms(q_ref[...], bs_ref[...], qg_ref[...])
        cos = jnp.concatenate([cos_ref[...]] * 4, axis=1)
        sin = jnp.concatenate([sin_ref[...]] * 4, axis=1)
        q = _rope(q, cos, sin) * (ATTN_HEAD_DIM ** -0.5)
        for p in range(4):
            qs_ref[p] = q[:, p * LANES:(p + 1) * LANES]

    qp = qs_ref[j]
    g = j // 2
    kd = kd_ref[g]
    vd = vd_ref[g]
    lane = lax.broadcasted_iota(jnp.int32, qp.shape, 1)
    low = lane < ATTN_HEAD_DIM
    outs = []
    for e in range(2):
        qm = jnp.where(low if e == 0 else jnp.logical_not(low), qp, 0.0).astype(BF16)
        s = _dot_nt(qm, kd)
        m = jnp.max(s, axis=-1, keepdims=True)
        p = jnp.exp(s - m)
        l = jnp.sum(p, axis=-1, keepdims=True)
        o = _dot(p.astype(BF16), vd)
        outs.append(o / l)
    o_pair = jnp.where(low, outs[0], outs[1])
    o_ref[...] = (o_pair * _silu(z_ref[...])).astype(o_ref.dtype)


def _attention(h, rope_q, rope_k, q_gain, k_gain, blocksum):
    bsz, seq, _ = h.shape
    tq = min(ATTN_TQ, seq)
    prep_rows = min(512, seq)
    cos2, sin2 = rope_q
    cosk, sink = rope_k
    kernel = functools.partial(_attn_kernel, seq=seq, tq=tq, prep_rows=prep_rows)
    return pl.pallas_call(
        kernel,
        grid=(bsz, seq // tq, 4),
        in_specs=[
            pl.BlockSpec((None, tq, GW), lambda b, n, j: (b, n, G_AQ)),
            pl.BlockSpec((None, tq, LANES), lambda b, n, j: (b, n, G_AZ * 4 + j)),
            pl.BlockSpec((None, seq, 2 * LANES), lambda b, n, j: (b, 0, G_AKV * 2)),
            pl.BlockSpec((tq, LANES), lambda b, n, j: (n, 0)),
            pl.BlockSpec((tq, LANES), lambda b, n, j: (n, 0)),
            pl.BlockSpec((seq, LANES), lambda b, n, j: (0, 0)),
            pl.BlockSpec((seq, LANES), lambda b, n, j: (0, 0)),
            pl.BlockSpec((1, GW), lambda b, n, j: (0, 0)),
            pl.BlockSpec((1, LANES), lambda b, n, j: (0, 0)),
            pl.BlockSpec((GW, GW), lambda b, n, j: (0, 0)),
        ],
        out_specs=pl.BlockSpec((None, tq, LANES), lambda b, n, j: (b, n, j)),
        out_shape=jax.ShapeDtypeStruct((bsz, seq, GW), BF16),
        scratch_shapes=[pltpu.VMEM((2, seq, LANES), BF16), pltpu.VMEM((2, seq, LANES), BF16),
                        pltpu.VMEM((4, tq, LANES), F32)],
        compiler_params=_params(("arbitrary", "arbitrary", "arbitrary")),
        name="attention",
    )(h, h, h, cos2, sin2, cosk, sink, q_gain, k_gain, blocksum)


def _scan_constants(chunk, reverse):
    t = np.arange(chunk)[:, None]
    u = np.arange(chunk)[None, :]
    if not reverse:
        tri = u <= t
        rest = u > t
    else:
        tri = u >= t
        rest = u < t
    blocks = [tri, rest]
    masks = []
    q_half_bits = []
    h = chunk // 2
    while h >= LIN_SUB:
        blk = t // (2 * h)
        half = (t // h) % 2
        if not reverse:
            ref = blk * 2 * h + h
            m = np.where(half == 1, (u > ref) & (u <= t), (u > t) & (u <= ref))
            q_half = 1
        else:
            ref = blk * 2 * h + h - 1
            m = np.where(half == 0, (u >= t) & (u < ref), (u >= ref) & (u < t))
            q_half = 0
        blocks.append(m)
        s_blk = u // (2 * h)
        s_half = (u // h) % 2
        masks.append((blk == s_blk) & (half == q_half) & (s_half == 1 - q_half))
        q_half_bits.append((h, q_half))
        h //= 2
    blocks.append(np.ones((SUBLANES, chunk), bool))
    stack = np.concatenate(blocks, axis=0).astype(np.float32)
    masks = np.stack(masks, axis=0).astype(np.float32)
    return jnp.asarray(stack, BF16), jnp.asarray(masks, F32), tuple(q_half_bits)


def _scan_kernel(*refs, mode, reverse, final, chunk, tb, levels):
    it = iter(refs)
    if mode == "hgrn":
        q_ref, f_ref, v_ref = next(it), next(it), next(it)
        loglb_ref, log1m_ref = next(it), next(it)
    else:
        q_ref, k_ref, v_ref, low_ref = next(it), next(it), next(it), next(it)
        w2_ref, gb_ref = next(it), next(it)
    if final:
        z_ref, of_ref, gain_ref = next(it), next(it), next(it)
    stack_ref, lmask_ref = next(it), next(it)
    o_ref = next(it)
    st_ref, qs_ref, ks_ref, gs_ref = next(it), next(it), next(it), next(it)
    if final:
        os_ref = next(it)

    n = pl.program_id(1)

    @pl.when(n == 0)
    def _reset():
        st_ref[...] = jnp.zeros_like(st_ref)

    if mode == "hgrn":
        qr = q_ref[...]
        qs_ref[...] = _silu(qr) * (HGRN_EXPAND ** -0.5)
        f = f_ref[...]
        ls = _log_sigmoid(f)
        a = loglb_ref[...]
        c = log1m_ref[...] + ls
        gs_ref[...] = jnp.maximum(a, c) + jnp.log1p(jnp.exp(-jnp.abs(a - c)))
        ks_ref[...] = jnp.exp(log1m_ref[...] + ls - f)
    else:
        qs_ref[...] = q_ref[...] * (GLA_HEAD_K ** -0.5)
        ks_ref[...] = k_ref[...]
        gk = _dot(low_ref[...].astype(BF16), w2_ref[...]) + gb_ref[...]
        gs_ref[...] = _log_sigmoid(gk) * (1.0 / GLA_GATE_NORMALIZER)

    nc = tb // chunk
    shift = (chunk - 1) if reverse else 1
    row = lax.broadcasted_iota(jnp.int32, (chunk, LANES), 0)
    rr = lax.broadcasted_iota(jnp.int32, (chunk, chunk), 0)
    cc = lax.broadcasted_iota(jnp.int32, (chunk, chunk), 1)
    diff = (cc - rr) if reverse else (rr - cc)
    sub_edge = (LIN_SUB - 1) if reverse else 0
    keep = (row & (LIN_SUB - 1)) != sub_edge
    end_row = 0 if reverse else chunk - 1
    n_lv = len(levels)

    for hh in range(4):
        lanes = slice(hh * LANES, (hh + 1) * LANES)

        def body(ci, carry, lanes=lanes, hh=hh):
            c = (nc - 1 - ci) if reverse else ci
            r0 = pl.multiple_of(c * chunk, chunk)
            q = qs_ref[pl.ds(r0, chunk), lanes]
            k = ks_ref[pl.ds(r0, chunk), lanes]
            g = gs_ref[pl.ds(r0, chunk), lanes]
            v = v_ref[pl.ds(r0, chunk), lanes].astype(BF16)
            st = st_ref[hh]

            sums = _sel_dot(stack_ref[...], g)
            b = sums[0:chunk]
            rest = sums[chunk:2 * chunk]
            total = sums[(2 + n_lv) * chunk:(2 + n_lv) * chunk + 1]

            o = _dot_nt((q * jnp.exp(b)).astype(BF16), st.astype(BF16))
            k_end = (k * jnp.exp(rest)).astype(BF16)
            st_ref[hh] = jnp.exp(total) * st + _dot_tn(v, k_end)

            scores = jnp.zeros((chunk, chunk), F32)
            for li, (h, q_half) in enumerate(levels):
                log_e = sums[(2 + li) * chunk:(3 + li) * chunk]
                is_q = ((row // h) & 1) == q_half
                x = (jnp.where(is_q, q, k) * jnp.exp(log_e)).astype(BF16)
                scores = scores + _dot_nt(x, x) * lmask_ref[li]

            f0 = jnp.where(keep, jnp.exp(g), 0.0)
            a = k
            band = jnp.where(diff == 0, jnp.sum(q * a, axis=-1, keepdims=True), 0.0)
            for d in range(1, LIN_SUB):
                a = f0 * pltpu.roll(a, shift, 0)
                band = jnp.where(diff == d, jnp.sum(q * a, axis=-1, keepdims=True), band)
            scores = scores + band
            o = o + _dot(scores.astype(BF16), v)
            if final:
                os_ref[pl.ds(r0, chunk), lanes] = o
            else:
                o_ref[pl.ds(r0, chunk), lanes] = o
            return carry

        lax.fori_loop(0, nc, body, 0)

    if final:
        o = os_ref[...] + of_ref[...]
        if mode == "hgrn":
            ms = jnp.mean(o * o, axis=-1, keepdims=True)
            y = o * lax.rsqrt(ms + 1e-6) * gain_ref[...]
        else:
            parts = []
            for hh in range(4):
                oh = o[:, hh * LANES:(hh + 1) * LANES]
                ms = jnp.mean(oh * oh, axis=-1, keepdims=True)
                parts.append(oh * lax.rsqrt(ms + 1e-6))
            y = jnp.concatenate(parts, axis=1) * gain_ref[...]
        o_ref[...] = (y * _silu(z_ref[...])).astype(o_ref.dtype)


def _gated_scan(h, mode, reverse, consts, params, o_fwd=None):
    bsz, seq, _ = h.shape
    tb = min(SCAN_TOKENS, seq)
    nb = seq // tb
    chunk = LIN_CHUNK
    stack, lmask, levels = consts
    final = reverse

    def blk(n):
        return (nb - 1 - n) if reverse else n

    def col(group):
        return pl.BlockSpec((None, tb, GW), lambda b, n: (b, blk(n), group))

    def full(arr):
        nd = arr.ndim
        return pl.BlockSpec(arr.shape, lambda b, n: (0,) * nd)

    if mode == "hgrn":
        log_lb, log_1m_lb, gain = params
        ins = [h, h, h, log_lb, log_1m_lb]
        specs = [col(G_HQ), col(G_HFB if reverse else G_HFF), col(G_HI), full(log_lb), full(log_1m_lb)]
        z_group = G_HZ
    else:
        w2, gb, gain = params
        ins = [h, h, h, h, w2, gb]
        specs = [col(G_GQ), col(G_GK), col(G_GV),
                 pl.BlockSpec((None, tb, LANES), lambda b, n: (b, blk(n), G_MISC * 4 + 1)),
                 full(w2), full(gb)]
        z_group = G_GZ
    if final:
        ins += [h, o_fwd, gain]
        specs += [col(z_group), pl.BlockSpec((None, tb, GW), lambda b, n: (b, blk(n), 0)), full(gain)]
    ins += [stack, lmask]
    specs += [full(stack), full(lmask)]
    scratch = [pltpu.VMEM((4, LANES, LANES), F32), pltpu.VMEM((tb, GW), F32),
               pltpu.VMEM((tb, GW), F32), pltpu.VMEM((tb, GW), F32)]
    if final:
        scratch.append(pltpu.VMEM((tb, GW), F32))
    kernel = functools.partial(_scan_kernel, mode=mode, reverse=reverse, final=final,
                               chunk=chunk, tb=tb, levels=levels)
    return pl.pallas_call(
        kernel,
        grid=(bsz, nb),
        in_specs=specs,
        out_specs=pl.BlockSpec((None, tb, GW), lambda b, n: (b, blk(n), 0)),
        out_shape=jax.ShapeDtypeStruct((bsz, seq, GW), BF16 if final else F32),
        scratch_shapes=scratch,
        compiler_params=_params(("arbitrary", "arbitrary")),
        name=f"{mode}_{'bwd' if reverse else 'fwd'}",
    )(*ins)


def _ssd_kernel(*refs, reverse, final, chunk, tb, nb):
    it = iter(refs)
    x_ref, xp_ref, xn_ref = next(it), next(it), next(it)
    bc_ref, bcp_ref, bcn_ref = next(it), next(it), next(it)
    dt_ref = next(it)
    cw_ref, cb_ref, dtb_ref, acoef_ref, expand_ref, tri_ref = (next(it) for _ in range(6))
    if final:
        z_ref, yf_ref, dskip_ref, gain_ref = next(it), next(it), next(it), next(it)
    o_ref = next(it)
    st_ref, ext_ref, u_ref, dts_ref, as_ref = (next(it) for _ in range(5))
    if final:
        ys_ref = next(it)

    n = pl.program_id(1)
    pos = (nb - 1 - n) if reverse else n

    @pl.when(n == 0)
    def _reset():
        st_ref[...] = jnp.zeros_like(st_ref)

    halves = ((x_ref, xp_ref, xn_ref), (bc_ref, bcp_ref, bcn_ref))
    for hi, (cur, prev, nxt) in enumerate(halves):
        cols = slice(hi * GW, (hi + 1) * GW)
        ext_ref[0:SUBLANES, :] = jnp.where(pos > 0, prev[...], 0.0)
        ext_ref[SUBLANES:SUBLANES + tb, :] = cur[...]
        ext_ref[SUBLANES + tb:2 * SUBLANES + tb, :] = jnp.where(pos < nb - 1, nxt[...], 0.0)
        acc = jnp.broadcast_to(cb_ref[:, cols], (tb, GW))
        for tap in range(SSD_CONV_WIDTH):
            acc = acc + cw_ref[tap:tap + 1, cols] * ext_ref[SUBLANES - 2 + tap:SUBLANES - 2 + tap + tb, :]
        u_ref[:, cols] = _silu(acc)

    dt = _softplus(dt_ref[...] + dtb_ref[...])
    dts_ref[...] = dt
    as_ref[...] = dt * acoef_ref[...]

    nc = tb // chunk
    head0 = SSD_HEADS if reverse else 0
    rr = lax.broadcasted_iota(jnp.int32, (chunk, chunk), 0)
    cc = lax.broadcasted_iota(jnp.int32, (chunk, chunk), 1)
    valid = (rr <= cc) if reverse else (rr >= cc)
    lane = lax.broadcasted_iota(jnp.int32, (chunk, LANES), 1)
    low = lane < SSD_HEAD_DIM
    end_row = 0 if reverse else chunk - 1

    def body(ci, carry):
        c = (nc - 1 - ci) if reverse else ci
        r0 = pl.multiple_of(c * chunk, chunk)
        dtc = dts_ref[pl.ds(r0, chunk), :]
        ac = as_ref[pl.ds(r0, chunk), :]
        acs = _sel_dot(tri_ref[...], ac)
        a_end = acs[end_row:end_row + 1, :]
        stacked = jnp.concatenate([dtc, acs, a_end - acs], axis=0)
        wide = _dot_sel(stacked, expand_ref[...])
        dt_x = wide[0:chunk]
        e_acs = jnp.exp(wide[chunk:2 * chunk])
        e_rest = jnp.exp(wide[2 * chunk:3 * chunk])
        e_end = e_acs[end_row:end_row + 1, :]
        xs = u_ref[pl.ds(r0, chunk), 0:GW]
        xdt = xs * dt_x
        wx = (e_rest * xdt).astype(BF16)
        acs_t = jnp.transpose(acs)
        for g in range(SSD_GROUPS):
            bg = u_ref[pl.ds(r0, chunk), GW + g * LANES:GW + (g + 1) * LANES].astype(BF16)
            cg = u_ref[pl.ds(r0, chunk), GW + (SSD_GROUPS + g) * LANES:
                       GW + (SSD_GROUPS + g + 1) * LANES].astype(BF16)
            gm = _dot_nt(cg, bg)
            gcols = slice(g * 2 * LANES, (g + 1) * 2 * LANES)
            st = st_ref[g]
            y_off = _dot(cg, st.astype(BF16)) * e_acs[:, gcols]
            st_ref[g] = e_end[:, gcols] * st + _dot_tn(bg, wx[:, gcols])
            for jp in range(2):
                pcols = slice((2 * g + jp) * LANES, (2 * g + jp + 1) * LANES)
                xp = xdt[:, pcols]
                acc = y_off[:, jp * LANES:(jp + 1) * LANES]
                for e in range(2):
                    hl = head0 + 4 * g + 2 * jp + e
                    dmat = acs[:, hl:hl + 1] - acs_t[hl:hl + 1, :]
                    lm = jnp.where(valid, jnp.exp(jnp.minimum(dmat, 0.0)), 0.0)
                    m = (gm * lm).astype(BF16)
                    xm = jnp.where(low if e == 0 else jnp.logical_not(low), xp, 0.0).astype(BF16)
                    acc = acc + _dot(m, xm)
                if final:
                    ys_ref[pl.ds(r0, chunk), pcols] = acc
                else:
                    o_ref[pl.ds(r0, chunk), pcols] = acc
        return carry

    lax.fori_loop(0, nc, body, 0)

    if final:
        y = ys_ref[...] + yf_ref[...] + dskip_ref[...] * u_ref[:, 0:GW]
        y = y * _silu(z_ref[...])
        ms = jnp.mean(y * y, axis=-1, keepdims=True)
        o_ref[...] = (y * lax.rsqrt(ms + 1e-6) * gain_ref[...]).astype(o_ref.dtype)


def _ssd(h, reverse, params, y_fwd=None):
    bsz, seq, _ = h.shape
    tb = min(SCAN_TOKENS, seq)
    nb = seq // tb
    chunk = min(SSD_CHUNK, tb)
    final = reverse
    conv_w, conv_b, dt_bias, a_coef, expand, tri, d_skip, gain = params
    hb = tb // SUBLANES
    nhb = seq // SUBLANES

    def blk(n):
        return (nb - 1 - n) if reverse else n

    def col(group):
        return pl.BlockSpec((None, tb, GW), lambda b, n: (b, blk(n), group))

    def prev(group):
        return pl.BlockSpec((None, SUBLANES, GW),
                            lambda b, n: (b, jnp.maximum(blk(n) * hb - 1, 0), group))

    def nxt(group):
        return pl.BlockSpec((None, SUBLANES, GW),
                            lambda b, n: (b, jnp.minimum((blk(n) + 1) * hb, nhb - 1), group))

    def full(arr):
        nd = arr.ndim
        return pl.BlockSpec(arr.shape, lambda b, n: (0,) * nd)

    ins = [h, h, h, h, h, h, h, conv_w, conv_b, dt_bias, a_coef, expand, tri]
    specs = [col(G_SX), prev(G_SX), nxt(G_SX), col(G_SBC), prev(G_SBC), nxt(G_SBC),
             pl.BlockSpec((None, tb, LANES), lambda b, n: (b, blk(n), G_MISC * 4)),
             full(conv_w), full(conv_b), full(dt_bias), full(a_coef), full(expand), full(tri)]
    if final:
        ins += [h, y_fwd, d_skip, gain]
        specs += [col(G_SZ), pl.BlockSpec((None, tb, GW), lambda b, n: (b, blk(n), 0)),
                  full(d_skip), full(gain)]
    scratch = [pltpu.VMEM((SSD_GROUPS, SSD_STATE, 2 * LANES), F32),
               pltpu.VMEM((tb + 2 * SUBLANES, GW), F32),
               pltpu.VMEM((tb, 2 * GW), F32),
               pltpu.VMEM((tb, LANES), F32), pltpu.VMEM((tb, LANES), F32)]
    if final:
        scratch.append(pltpu.VMEM((tb, GW), F32))
    kernel = functools.partial(_ssd_kernel, reverse=reverse, final=final, chunk=chunk, tb=tb, nb=nb)
    return pl.pallas_call(
        kernel,
        grid=(bsz, nb),
        in_specs=specs,
        out_specs=pl.BlockSpec((None, tb, GW), lambda b, n: (b, blk(n), 0)),
        out_shape=jax.ShapeDtypeStruct((bsz, seq, GW), BF16 if final else F32),
        scratch_shapes=scratch,
        compiler_params=_params(("arbitrary", "arbitrary")),
        name=f"ssd_{'bwd' if reverse else 'fwd'}",
    )(*ins)


def _out_kernel(ya_ref, yh_ref, ys_ref, yg_ref, x_ref, w_ref, g_ref, b_ref, xo_ref, xb_ref):
    acc = _dot(ya_ref[...], w_ref[0:GW, :])
    acc = acc + _dot(yh_ref[...], w_ref[GW:2 * GW, :])
    acc = acc + _dot(ys_ref[...], w_ref[2 * GW:3 * GW, :])
    acc = acc + _dot(yg_ref[...], w_ref[3 * GW:4 * GW, :])
    r = DEEPNORM_ALPHA * x_ref[...] + acc
    mu = jnp.mean(r, axis=-1, keepdims=True)
    rc = r - mu
    var = jnp.mean(rc * rc, axis=-1, keepdims=True)
    y = rc * lax.rsqrt(var + 1e-5) * g_ref[...] + b_ref[...]
    xo_ref[...] = y
    xb_ref[...] = y.astype(BF16)


def _out_proj_ln(ya, yh, ys, yg, x, w, ln_g, ln_b):
    t, d = x.shape
    tm = min(512, t)
    row = lambda width: pl.BlockSpec((tm, width), lambda i: (i, 0))
    const = lambda arr: pl.BlockSpec(arr.shape, lambda i: (0, 0))
    return pl.pallas_call(
        _out_kernel,
        grid=(t // tm,),
        in_specs=[row(GW), row(GW), row(GW), row(GW), row(d), const(w), const(ln_g), const(ln_b)],
        out_specs=[row(d), row(d)],
        out_shape=[jax.ShapeDtypeStruct((t, d), F32), jax.ShapeDtypeStruct((t, d), BF16)],
        compiler_params=_params(("parallel",)),
        name="out_proj_ln",
    )(ya, yh, ys, yg, x, w, ln_g, ln_b)


def _pad_heads(w, heads, width):
    d = w.shape[0]
    w = w.reshape(d, heads, width)
    w = jnp.pad(w, ((0, 0), (0, 0), (0, LANES - width)))
    return w.reshape(d, heads * LANES)


def _layout_w_in(w):
    d = w.shape[0]
    sizes = (512, 128, 128, 512, 512, 512, 512, 512, 512, 1024, 8, 8, 512, 256, 256, 512, 16, 16, 512)
    offs = np.concatenate([[0], np.cumsum(sizes)])
    (a_q, a_k, a_v, a_z, h_q, h_ff, h_fb, h_i, h_z,
     s_xbc, s_dtf, s_dtb, s_z, g_q, g_k, g_v, g_lf, g_lb, g_z) = [
        w[:, int(offs[i]):int(offs[i + 1])] for i in range(len(sizes))]
    zeros = lambda n: jnp.zeros((d, n), w.dtype)
    groups = [
        a_q, a_z, jnp.concatenate([a_k, a_v, zeros(256)], axis=1),
        h_q, h_ff, h_fb, h_i, h_z,
        s_xbc[:, :512], s_xbc[:, 512:], s_z,
        _pad_heads(g_q, GLA_HEADS, GLA_HEAD_K), _pad_heads(g_k, GLA_HEADS, GLA_HEAD_K), g_v, g_z,
        jnp.concatenate([s_dtf, s_dtb, zeros(112), g_lf, g_lb, zeros(96), zeros(256)], axis=1),
    ]
    return jnp.concatenate(groups, axis=1).astype(BF16)


def _rope_tables(seq):
    rows = seq // GRID_W
    row_pos = jnp.repeat(jnp.arange(rows, dtype=F32), GRID_W)
    col_pos = jnp.tile(jnp.arange(GRID_W, dtype=F32), rows)
    axis_dim = ATTN_HEAD_DIM // 2
    inv_freq = jnp.power(ROPE_THETA, -jnp.arange(0, axis_dim, 2, dtype=F32) / axis_dim)
    ang_r = row_pos[:, None] * inv_freq
    ang_c = col_pos[:, None] * inv_freq
    cos = jnp.concatenate([jnp.cos(ang_r)] * 2 + [jnp.cos(ang_c)] * 2, axis=1)
    sin = jnp.concatenate([-jnp.sin(ang_r), jnp.sin(ang_r), -jnp.sin(ang_c), jnp.sin(ang_c)], axis=1)
    return jnp.tile(cos, (1, 2)), jnp.tile(sin, (1, 2))


def _ssd_expand(reverse):
    m = np.zeros((LANES, GW), np.float32)
    for hd in range(SSD_HEADS):
        m[hd + (SSD_HEADS if reverse else 0), hd * SSD_HEAD_DIM:(hd + 1) * SSD_HEAD_DIM] = 1.0
    return jnp.asarray(m, BF16)


def _tri(chunk, reverse):
    t = np.arange(chunk)[:, None]
    u = np.arange(chunk)[None, :]
    return jnp.asarray((u >= t) if reverse else (u <= t), BF16)


def kernel(x, w_in, attn_q_norm, attn_k_norm, hgrn_lb_logits, hgrn_norm, ssd_conv_w, ssd_conv_b,
           ssd_dt_bias, ssd_a_log, ssd_d, ssd_norm, gla_gk_w2, gla_gk_b, gla_norm, w_out, ln_g, ln_b):
    bsz, seq, d = x.shape
    t = bsz * seq
    rope = _rope_tables(seq)
    blocksum = jnp.asarray(np.kron(np.eye(GW // ATTN_HEAD_DIM), np.ones((ATTN_HEAD_DIM, ATTN_HEAD_DIM))), BF16)
    lower_bounds = jnp.cumsum(jax.nn.softmax(hgrn_lb_logits.astype(F32), axis=0), axis=0)
    lower_bounds = lower_bounds - lower_bounds[0]
    scan_consts = {rev: _scan_constants(LIN_CHUNK, rev) for rev in (False, True)}
    ssd_chunk = min(SSD_CHUNK, min(SCAN_TOKENS, seq))
    ssd_consts = {rev: (_ssd_expand(rev), _tri(ssd_chunk, rev)) for rev in (False, True)}

    xf = x.reshape(t, d)
    xb = xf.astype(BF16)
    for i in range(DEPTH):
        h = _in_proj(xb, _layout_w_in(w_in[i])).reshape(bsz, seq, N_PROJ)

        y_a = _attention(h, rope, rope, jnp.tile(attn_q_norm[i], ATTN_HEADS)[None, :].astype(F32),
                         jnp.tile(attn_k_norm[i], 2)[None, :].astype(F32), blocksum)

        lb = jnp.maximum(lower_bounds[i], 0.0)[None, :]
        hgrn_params = (jnp.log(lb), jnp.log1p(-lb), hgrn_norm[i][None, :].astype(F32))
        o_f = _gated_scan(h, "hgrn", False, scan_consts[False], hgrn_params)
        y_h = _gated_scan(h, "hgrn", True, scan_consts[True], hgrn_params, o_f)

        gla_out = None
        for rev in (False, True):
            w2 = jnp.zeros((LANES, GW), F32)
            w2 = w2.at[rev * GLA_GATE_RANK:(rev + 1) * GLA_GATE_RANK].set(
                _pad_heads(gla_gk_w2[i, int(rev)], GLA_HEADS, GLA_HEAD_K))
            gb = _pad_heads(gla_gk_b[i, int(rev)][None, :], GLA_HEADS, GLA_HEAD_K)
            gla_params = (w2.astype(BF16), gb.astype(F32), jnp.tile(gla_norm[i], GLA_HEADS)[None, :].astype(F32))
            gla_out = _gated_scan(h, "gla", rev, scan_consts[rev], gla_params, gla_out)
        y_g = gla_out

        ssd_out = None
        for rev in (False, True):
            dt_bias = jnp.zeros((1, LANES), F32).at[0, :2 * SSD_HEADS].set(ssd_dt_bias[i].reshape(-1))
            a_coef = jnp.zeros((1, LANES), F32).at[0, :2 * SSD_HEADS].set(-jnp.exp(ssd_a_log[i].astype(F32)).reshape(-1))
            expand, tri = ssd_consts[rev]
            ssd_params = (ssd_conv_w[i].astype(F32), ssd_conv_b[i][None, :].astype(F32), dt_bias, a_coef,
                          expand, tri, jnp.repeat(ssd_d[i].astype(F32), SSD_HEAD_DIM)[None, :],
                          ssd_norm[i][None, :].astype(F32))
            ssd_out = _ssd(h, rev, ssd_params, ssd_out)
        y_s = ssd_out

        flat = lambda a: a.reshape(t, GW)
        xf, xb = _out_proj_ln(flat(y_a), flat(y_h), flat(y_s), flat(y_g), xf, w_out[i].astype(BF16),
                              ln_g[i][None, :].astype(F32), ln_b[i][None, :].astype(F32))
    return xf.reshape(bsz, seq, d)
```

```python
import functools
import math

import numpy as np
import jax
import jax.numpy as jnp
from jax import lax
from jax.experimental import pallas as pl
from jax.experimental.pallas import tpu as pltpu

F32 = jnp.float32
BF16 = jnp.bfloat16

D_MODEL = 1024
DEPTH = 2
GROUP_WIDTH = 512
GRID_W = 64
ATTN_HEADS = 8
ATTN_KV_HEADS = 2
ATTN_HEAD_DIM = 64
ROPE_THETA = 10000.0
HGRN_HEADS = 4
HGRN_EXPAND = 128
SSD_HEADS = 8
SSD_HEAD_DIM = 64
SSD_GROUPS = 2
SSD_STATE = 128
SSD_CONV_WIDTH = 5
GLA_HEADS = 4
GLA_HEAD_K = 64
GLA_HEAD_V = 128
GLA_GATE_RANK = 16
GLA_GATE_NORMALIZER = 16.0
DEEPNORM_ALPHA = (2 * DEPTH) ** 0.25

LANES = 128
SUBLANES = 8
GW = GROUP_WIDTH

(G_AQ, G_AZ, G_AKV, G_HQ, G_HFF, G_HFB, G_HI, G_HZ,
 G_SX, G_SBC, G_SZ, G_GQ, G_GK, G_GV, G_GZ, G_MISC) = range(16)
N_GROUPS = 16
N_PROJ = N_GROUPS * GW

LIN_CHUNK = 64
LIN_SUB = 4
SSD_CHUNK = 128
SCAN_TOKENS = 512
SCAN_PRE_ROWS = 128
ATTN_TQ = 256
VMEM_LIMIT = 48 * 1024 * 1024


def _dot(a, b):
    return jnp.dot(a, b, preferred_element_type=F32)


def _dot_nt(a, b):
    return lax.dot_general(a, b, (((1,), (1,)), ((), ())), preferred_element_type=F32)


def _dot_tn(a, b):
    return lax.dot_general(a, b, (((0,), (0,)), ((), ())), preferred_element_type=F32)


def _split3(x):
    hi = x.astype(BF16)
    r1 = x - hi.astype(F32)
    mid = r1.astype(BF16)
    lo = (r1 - mid.astype(F32)).astype(BF16)
    return hi, mid, lo


def _sel_dot(sel, x):
    hi, mid, lo = _split3(x)
    return _dot(sel, hi) + _dot(sel, mid) + _dot(sel, lo)


def _dot_sel(x, sel):
    hi, mid, lo = _split3(x)
    return _dot(hi, sel) + _dot(mid, sel) + _dot(lo, sel)


def _log_sigmoid(x):
    return jnp.minimum(x, 0.0) - jnp.log1p(jnp.exp(-jnp.abs(x)))


def _silu(x):
    return x * jax.nn.sigmoid(x)


def _softplus(x):
    return jnp.maximum(x, 0.0) + jnp.log1p(jnp.exp(-jnp.abs(x)))


def _params(sem):
    return pltpu.CompilerParams(dimension_semantics=sem, vmem_limit_bytes=VMEM_LIMIT)


def _matmul_kernel(x_ref, w_ref, o_ref):
    o_ref[...] = _dot(x_ref[...], w_ref[...])


def _in_proj(xb, w):
    t, d = xb.shape
    n = w.shape[1]
    tm = min(1024, t)
    tn = GW
    return pl.pallas_call(
        _matmul_kernel,
        grid=(t // tm, n // tn),
        in_specs=[pl.BlockSpec((tm, d), lambda i, j: (i, 0)),
                  pl.BlockSpec((d, tn), lambda i, j: (0, j))],
        out_specs=pl.BlockSpec((tm, tn), lambda i, j: (i, j)),
        out_shape=jax.ShapeDtypeStruct((t, n), F32),
        compiler_params=_params(("parallel", "parallel")),
        name="in_proj",
    )(xb, w)


def _head_rms(x, blocksum, gain, eps=1e-6):
    xx = x * x
    hi = xx.astype(BF16)
    lo = (xx - hi.astype(F32)).astype(BF16)
    ssq = _dot(hi, blocksum) + _dot(lo, blocksum)
    return x * lax.rsqrt(ssq * (1.0 / ATTN_HEAD_DIM) + eps) * gain


def _rope(x, cos, sin_signed):
    w = x.shape[1]
    lane = lax.broadcasted_iota(jnp.int32, x.shape, 1)
    first = (lane & 31) < 16
    partner = jnp.where(first, pltpu.roll(x, w - 16, 1), pltpu.roll(x, 16, 1))
    return x * cos + partner * sin_signed


def _attn_kernel(q_ref, z_ref, kv_ref, cos_ref, sin_ref, cosk_ref, sink_ref, qg_ref, kg_ref,
                 bs_ref, o_ref, kd_ref, vd_ref, *, seq, prep_rows):
    n = pl.program_id(1)

    @pl.when(n == 0)
    def _prep_kv():
        def body(i, carry):
            r0 = pl.multiple_of(i * prep_rows, prep_rows)
            kv = kv_ref[pl.ds(r0, prep_rows), :]
            k = kv[:, :LANES]
            v = kv[:, LANES:]
            k = _head_rms(k, bs_ref[0:LANES, 0:LANES], kg_ref[...])
            k = _rope(k, cosk_ref[pl.ds(r0, prep_rows), :], sink_ref[pl.ds(r0, prep_rows), :])
            lane = lax.broadcasted_iota(jnp.int32, k.shape, 1)
            low = lane < ATTN_HEAD_DIM
            k_sw = pltpu.roll(k, ATTN_HEAD_DIM, 1)
            v_sw = pltpu.roll(v, ATTN_HEAD_DIM, 1)
            kd_ref[0, pl.ds(r0, prep_rows), :] = jnp.where(low, k, k_sw).astype(BF16)
            kd_ref[1, pl.ds(r0, prep_rows), :] = jnp.where(low, k_sw, k).astype(BF16)
            vd_ref[0, pl.ds(r0, prep_rows), :] = jnp.where(low, v, v_sw).astype(BF16)
            vd_ref[1, pl.ds(r0, prep_rows), :] = jnp.where(low, v_sw, v).astype(BF16)
            return carry
        lax.fori_loop(0, seq // prep_rows, body, 0)

    q = _head_rms(q_ref[...], bs_ref[...], qg_ref[...])
    cos = jnp.concatenate([cos_ref[...]] * 4, axis=1)
    sin = jnp.concatenate([sin_ref[...]] * 4, axis=1)
    q = _rope(q, cos, sin) * (ATTN_HEAD_DIM ** -0.5)
    lane = lax.broadcasted_iota(jnp.int32, (q.shape[0], LANES), 1)
    low = lane < ATTN_HEAD_DIM
    group = ATTN_HEADS // ATTN_KV_HEADS

    def scores(hd):
        pair, e = divmod(hd, 2)
        qp = q[:, pair * LANES:(pair + 1) * LANES]
        qm = jnp.where(low if e == 0 else jnp.logical_not(low), qp, 0.0).astype(BF16)
        return _dot_nt(qm, kd_ref[hd // group])

    outs = []
    s_next = scores(0)
    for hd in range(ATTN_HEADS):
        s = s_next
        if hd + 1 < ATTN_HEADS:
            s_next = scores(hd + 1)
        m = jnp.max(s, axis=-1, keepdims=True)
        p = jnp.exp(s - m)
        l = jnp.sum(p, axis=-1, keepdims=True)
        o = _dot(p.astype(BF16), vd_ref[hd // group])
        outs.append(o / l)
    for pair in range(ATTN_HEADS // 2):
        cols = slice(pair * LANES, (pair + 1) * LANES)
        o_pair = jnp.where(low, outs[2 * pair], outs[2 * pair + 1])
        o_ref[:, cols] = (o_pair * _silu(z_ref[:, cols])).astype(o_ref.dtype)


def _attention(h, rope_q, rope_k, q_gain, k_gain, blocksum):
    bsz, seq, _ = h.shape
    tq = min(ATTN_TQ, seq)
    prep_rows = min(512, seq)
    cos2, sin2 = rope_q
    cosk, sink = rope_k
    kernel = functools.partial(_attn_kernel, seq=seq, prep_rows=prep_rows)
    return pl.pallas_call(
        kernel,
        grid=(bsz, seq // tq),
        in_specs=[
            pl.BlockSpec((None, tq, GW), lambda b, n: (b, n, G_AQ)),
            pl.BlockSpec((None, tq, GW), lambda b, n: (b, n, G_AZ)),
            pl.BlockSpec((None, seq, 2 * LANES), lambda b, n: (b, 0, G_AKV * 2)),
            pl.BlockSpec((tq, LANES), lambda b, n: (n, 0)),
            pl.BlockSpec((tq, LANES), lambda b, n: (n, 0)),
            pl.BlockSpec((seq, LANES), lambda b, n: (0, 0)),
            pl.BlockSpec((seq, LANES), lambda b, n: (0, 0)),
            pl.BlockSpec((1, GW), lambda b, n: (0, 0)),
            pl.BlockSpec((1, LANES), lambda b, n: (0, 0)),
            pl.BlockSpec((GW, GW), lambda b, n: (0, 0)),
        ],
        out_specs=pl.BlockSpec((None, tq, GW), lambda b, n: (b, n, 0)),
        out_shape=jax.ShapeDtypeStruct((bsz, seq, GW), BF16),
        scratch_shapes=[pltpu.VMEM((2, seq, LANES), BF16), pltpu.VMEM((2, seq, LANES), BF16)],
        compiler_params=_params(("arbitrary", "arbitrary")),
        name="attention",
    )(h, h, h, cos2, sin2, cosk, sink, q_gain, k_gain, blocksum)


def _scan_constants(chunk, tb, reverse):
    t = np.arange(chunk)[:, None]
    u = np.arange(chunk)[None, :]
    tri = (u >= t) if reverse else (u <= t)
    q_half = 0 if reverse else 1
    masks = []
    levels = []
    h = chunk // 2
    while h >= LIN_SUB:
        same_block = (t // (2 * h)) == (u // (2 * h))
        masks.append(same_block & ((t // h) % 2 == q_half) & ((u // h) % 2 == 1 - q_half))
        levels.append((h, q_half))
        h //= 2
    masks = np.stack(masks, axis=0).astype(np.float32)
    tri_blocks = np.kron(np.eye(tb // chunk), tri)
    return jnp.asarray(tri_blocks, BF16), jnp.asarray(masks, F32), tuple(levels)


def _scan_kernel(*refs, mode, reverse, final, chunk, tb, levels, pre_rows):
    it = iter(refs)
    if mode == "hgrn":
        q_ref, f_ref, v_ref = next(it), next(it), next(it)
        loglb_ref, log1m_ref = next(it), next(it)
    else:
        q_ref, k_ref, v_ref, low_ref = next(it), next(it), next(it), next(it)
        w2_ref, gb_ref = next(it), next(it)
    if final:
        z_ref, of_ref, gain_ref = next(it), next(it), next(it)
    stack_ref, lmask_ref = next(it), next(it)
    o_ref = next(it)
    st_ref, qs_ref, ks_ref, gs_ref, bs_ref = (next(it) for _ in range(5))
    if final:
        os_ref = next(it)

    n = pl.program_id(1)

    @pl.when(n == 0)
    def _reset():
        st_ref[...] = jnp.zeros_like(st_ref)

    if mode == "gla":
        gs_ref[...] = _dot(low_ref[...].astype(BF16), w2_ref[...]) + gb_ref[...]
    for r in range(0, tb, pre_rows):
        rows = slice(r, r + pre_rows)
        if mode == "hgrn":
            qr = q_ref[rows, :]
            qs_ref[rows, :] = _silu(qr) * (HGRN_EXPAND ** -0.5)
            f = f_ref[rows, :]
            ls = _log_sigmoid(f)
            a = loglb_ref[...]
            c = log1m_ref[...] + ls
            g_rows = jnp.maximum(a, c) + jnp.log1p(jnp.exp(-jnp.abs(a - c)))
            ks_ref[rows, :] = jnp.exp(log1m_ref[...] + ls - f)
        else:
            qs_ref[rows, :] = q_ref[rows, :] * (GLA_HEAD_K ** -0.5)
            ks_ref[rows, :] = k_ref[rows, :]
            g_rows = _log_sigmoid(gs_ref[rows, :]) * (1.0 / GLA_GATE_NORMALIZER)
        gs_ref[rows, :] = g_rows
        bs_ref[rows, :] = _sel_dot(stack_ref[...], g_rows)

    nc = tb // chunk
    shift = (chunk - 1) if reverse else 1
    row = lax.broadcasted_iota(jnp.int32, (chunk, LANES), 0)
    rr = lax.broadcasted_iota(jnp.int32, (chunk, chunk), 0)
    cc = lax.broadcasted_iota(jnp.int32, (chunk, chunk), 1)
    diff = (cc - rr) if reverse else (rr - cc)
    sub_edge = (LIN_SUB - 1) if reverse else 0
    keep = (row & (LIN_SUB - 1)) != sub_edge
    end_row = 0 if reverse else chunk - 1

    def body(ci, carry):
        c = (nc - 1 - ci) if reverse else ci
        r0 = pl.multiple_of(c * chunk, chunk)
        heads = range(4)
        lanes = [slice(hh * LANES, (hh + 1) * LANES) for hh in heads]
        q = [qs_ref[pl.ds(r0, chunk), ln] for ln in lanes]
        k = [ks_ref[pl.ds(r0, chunk), ln] for ln in lanes]
        g = [gs_ref[pl.ds(r0, chunk), ln] for ln in lanes]
        v = [v_ref[pl.ds(r0, chunk), ln].astype(BF16) for ln in lanes]
        b = [bs_ref[pl.ds(r0, chunk), ln] for ln in lanes]

        o = []
        for hh in heads:
            st = st_ref[hh]
            total = b[hh][end_row:end_row + 1, :]
            o.append(_dot_nt((q[hh] * jnp.exp(b[hh])).astype(BF16), st.astype(BF16)))
            k_end = (k[hh] * jnp.exp(total - b[hh])).astype(BF16)
            st_ref[hh] = jnp.exp(total) * st + _dot_tn(v[hh], k_end)

        scores = [jnp.zeros((chunk, chunk), F32) for _ in heads]
        for li, (h, q_half) in enumerate(levels):
            is_q = ((row // h) & 1) == q_half
            ref_row = h - 1 if reverse else h
            for hh in heads:
                b3 = b[hh].reshape(chunk // (2 * h), 2 * h, LANES)
                d3 = b3 - b3[:, ref_row:ref_row + 1, :]
                dlt = d3.reshape(chunk, LANES)
                log_e = jnp.where(is_q, dlt, -dlt)
                x = (jnp.where(is_q, q[hh], k[hh]) * jnp.exp(log_e)).astype(BF16)
                scores[hh] = scores[hh] + _dot_nt(x, x) * lmask_ref[li]

        for hh in heads:
            f0 = jnp.where(keep, jnp.exp(g[hh]), 0.0)
            a = k[hh]
            band = jnp.where(diff == 0, jnp.sum(q[hh] * a, axis=-1, keepdims=True), 0.0)
            for d in range(1, LIN_SUB):
                a = f0 * pltpu.roll(a, shift, 0)
                band = jnp.where(diff == d, jnp.sum(q[hh] * a, axis=-1, keepdims=True), band)
            scores[hh] = scores[hh] + band

        for hh in heads:
            out = o[hh] + _dot(scores[hh].astype(BF16), v[hh])
            if final:
                os_ref[pl.ds(r0, chunk), lanes[hh]] = out
            else:
                o_ref[pl.ds(r0, chunk), lanes[hh]] = out
        return carry

    lax.fori_loop(0, nc, body, 0, unroll=2)

    if final:
        o = os_ref[...] + of_ref[...]
        if mode == "hgrn":
            ms = jnp.mean(o * o, axis=-1, keepdims=True)
            y = o * lax.rsqrt(ms + 1e-6) * gain_ref[...]
        else:
            parts = []
            for hh in range(4):
                oh = o[:, hh * LANES:(hh + 1) * LANES]
                ms = jnp.mean(oh * oh, axis=-1, keepdims=True)
                parts.append(oh * lax.rsqrt(ms + 1e-6))
            y = jnp.concatenate(parts, axis=1) * gain_ref[...]
        o_ref[...] = (y * _silu(z_ref[...])).astype(o_ref.dtype)


def _gated_scan(h, mode, reverse, consts, params, o_fwd=None):
    bsz, seq, _ = h.shape
    tb = min(SCAN_TOKENS, seq)
    nb = seq // tb
    chunk = LIN_CHUNK
    stack, lmask, levels = consts
    final = reverse

    def blk(n):
        return (nb - 1 - n) if reverse else n

    def col(group):
        return pl.BlockSpec((None, tb, GW), lambda b, n: (b, blk(n), group))

    def full(arr):
        nd = arr.ndim
        return pl.BlockSpec(arr.shape, lambda b, n: (0,) * nd)

    if mode == "hgrn":
        log_lb, log_1m_lb, gain = params
        ins = [h, h, h, log_lb, log_1m_lb]
        specs = [col(G_HQ), col(G_HFB if reverse else G_HFF), col(G_HI), full(log_lb), full(log_1m_lb)]
        z_group = G_HZ
    else:
        w2, gb, gain = params
        ins = [h, h, h, h, w2, gb]
        specs = [col(G_GQ), col(G_GK), col(G_GV),
                 pl.BlockSpec((None, tb, LANES), lambda b, n: (b, blk(n), G_MISC * 4 + 1)),
                 full(w2), full(gb)]
        z_group = G_GZ
    if final:
        ins += [h, o_fwd, gain]
        specs += [col(z_group), pl.BlockSpec((None, tb, GW), lambda b, n: (b, blk(n), 0)), full(gain)]
    ins += [stack, lmask]
    specs += [full(stack), full(lmask)]
    scratch = [pltpu.VMEM((4, LANES, LANES), F32)] + [pltpu.VMEM((tb, GW), F32)] * 4
    if final:
        scratch.append(pltpu.VMEM((tb, GW), F32))
    kernel = functools.partial(_scan_kernel, mode=mode, reverse=reverse, final=final,
                               chunk=chunk, tb=tb, levels=levels, pre_rows=stack.shape[0])
    return pl.pallas_call(
        kernel,
        grid=(bsz, nb),
        in_specs=specs,
        out_specs=pl.BlockSpec((None, tb, GW), lambda b, n: (b, blk(n), 0)),
        out_shape=jax.ShapeDtypeStruct((bsz, seq, GW), BF16 if final else F32),
        scratch_shapes=scratch,
        compiler_params=_params(("arbitrary", "arbitrary")),
        name=f"{mode}_{'bwd' if reverse else 'fwd'}",
    )(*ins)


def _ssd_kernel(*refs, reverse, final, chunk, tb, nb):
    it = iter(refs)
    x_ref, xp_ref, xn_ref = next(it), next(it), next(it)
    bc_ref, bcp_ref, bcn_ref = next(it), next(it), next(it)
    dt_ref = next(it)
    cw_ref, cb_ref, dtb_ref, acoef_ref, expand_ref, tri_ref = (next(it) for _ in range(6))
    if final:
        z_ref, yf_ref, dskip_ref, gain_ref = next(it), next(it), next(it), next(it)
    o_ref = next(it)
    st_ref, ext_ref, u_ref, dts_ref, as_ref = (next(it) for _ in range(5))
    if final:
        ys_ref = next(it)

    n = pl.program_id(1)
    pos = (nb - 1 - n) if reverse else n

    @pl.when(n == 0)
    def _reset():
        st_ref[...] = jnp.zeros_like(st_ref)

    halves = ((x_ref, xp_ref, xn_ref), (bc_ref, bcp_ref, bcn_ref))
    for hi, (cur, prev, nxt) in enumerate(halves):
        cols = slice(hi * GW, (hi + 1) * GW)
        ext_ref[0:SUBLANES, :] = jnp.where(pos > 0, prev[...], 0.0)
        ext_ref[SUBLANES:SUBLANES + tb, :] = cur[...]
        ext_ref[SUBLANES + tb:2 * SUBLANES + tb, :] = jnp.where(pos < nb - 1, nxt[...], 0.0)
        acc = jnp.broadcast_to(cb_ref[:, cols], (tb, GW))
        for tap in range(SSD_CONV_WIDTH):
            acc = acc + cw_ref[tap:tap + 1, cols] * ext_ref[SUBLANES - 2 + tap:SUBLANES - 2 + tap + tb, :]
        u_ref[:, cols] = _silu(acc)

    dt = _softplus(dt_ref[...] + dtb_ref[...])
    dts_ref[...] = dt
    as_ref[...] = dt * acoef_ref[...]

    nc = tb // chunk
    head0 = SSD_HEADS if reverse else 0
    rr = lax.broadcasted_iota(jnp.int32, (chunk, chunk), 0)
    cc = lax.broadcasted_iota(jnp.int32, (chunk, chunk), 1)
    valid = (rr <= cc) if reverse else (rr >= cc)
    lane = lax.broadcasted_iota(jnp.int32, (chunk, LANES), 1)
    low = lane < SSD_HEAD_DIM
    end_row = 0 if reverse else chunk - 1

    def body(ci, carry):
        c = (nc - 1 - ci) if reverse else ci
        r0 = pl.multiple_of(c * chunk, chunk)
        dtc = dts_ref[pl.ds(r0, chunk), :]
        ac = as_ref[pl.ds(r0, chunk), :]
        acs = _sel_dot(tri_ref[...], ac)
        a_end = acs[end_row:end_row + 1, :]
        stacked = jnp.concatenate([dtc, acs, a_end - acs], axis=0)
        wide = _dot_sel(stacked, expand_ref[...])
        dt_x = wide[0:chunk]
        e_acs = jnp.exp(wide[chunk:2 * chunk])
        e_rest = jnp.exp(wide[2 * chunk:3 * chunk])
        e_end = e_acs[end_row:end_row + 1, :]
        xs = u_ref[pl.ds(r0, chunk), 0:GW]
        xdt = xs * dt_x
        wx = (e_rest * xdt).astype(BF16)
        acs_t = jnp.transpose(acs)
        for g in range(SSD_GROUPS):
            bg = u_ref[pl.ds(r0, chunk), GW + g * LANES:GW + (g + 1) * LANES].astype(BF16)
            cg = u_ref[pl.ds(r0, chunk), GW + (SSD_GROUPS + g) * LANES:
                       GW + (SSD_GROUPS + g + 1) * LANES].astype(BF16)
            gm = _dot_nt(cg, bg)
            gcols = slice(g * 2 * LANES, (g + 1) * 2 * LANES)
            st = st_ref[g]
            y_off = _dot(cg, st.astype(BF16)) * e_acs[:, gcols]
            st_ref[g] = e_end[:, gcols] * st + _dot_tn(bg, wx[:, gcols])
            for jp in range(2):
                pcols = slice((2 * g + jp) * LANES, (2 * g + jp + 1) * LANES)
                xp = xdt[:, pcols]
                acc = y_off[:, jp * LANES:(jp + 1) * LANES]
                for e in range(2):
                    hl = head0 + 4 * g + 2 * jp + e
                    dmat = acs[:, hl:hl + 1] - acs_t[hl:hl + 1, :]
                    lm = jnp.where(valid, jnp.exp(jnp.minimum(dmat, 0.0)), 0.0)
                    m = (gm * lm).astype(BF16)
                    xm = jnp.where(low if e == 0 else jnp.logical_not(low), xp, 0.0).astype(BF16)
                    acc = acc + _dot(m, xm)
                if final:
                    ys_ref[pl.ds(r0, chunk), pcols] = acc
                else:
                    o_ref[pl.ds(r0, chunk), pcols] = acc
        return carry

    lax.fori_loop(0, nc, body, 0)

    if final:
        y = ys_ref[...] + yf_ref[...] + dskip_ref[...] * u_ref[:, 0:GW]
        y = y * _silu(z_ref[...])
        ms = jnp.mean(y * y, axis=-1, keepdims=True)
        o_ref[...] = (y * lax.rsqrt(ms + 1e-6) * gain_ref[...]).astype(o_ref.dtype)


def _ssd(h, reverse, params, y_fwd=None):
    bsz, seq, _ = h.shape
    tb = min(SCAN_TOKENS, seq)
    nb = seq // tb
    chunk = min(SSD_CHUNK, tb)
    final = reverse
    conv_w, conv_b, dt_bias, a_coef, expand, tri, d_skip, gain = params
    hb = tb // SUBLANES
    nhb = seq // SUBLANES

    def blk(n):
        return (nb - 1 - n) if reverse else n

    def col(group):
        return pl.BlockSpec((None, tb, GW), lambda b, n: (b, blk(n), group))

    def prev(group):
        return pl.BlockSpec((None, SUBLANES, GW),
                            lambda b, n: (b, jnp.maximum(blk(n) * hb - 1, 0), group))

    def nxt(group):
        return pl.BlockSpec((None, SUBLANES, GW),
                            lambda b, n: (b, jnp.minimum((blk(n) + 1) * hb, nhb - 1), group))

    def full(arr):
        nd = arr.ndim
        return pl.BlockSpec(arr.shape, lambda b, n: (0,) * nd)

    ins = [h, h, h, h, h, h, h, conv_w, conv_b, dt_bias, a_coef, expand, tri]
    specs = [col(G_SX), prev(G_SX), nxt(G_SX), col(G_SBC), prev(G_SBC), nxt(G_SBC),
             pl.BlockSpec((None, tb, LANES), lambda b, n: (b, blk(n), G_MISC * 4)),
             full(conv_w), full(conv_b), full(dt_bias), full(a_coef), full(expand), full(tri)]
    if final:
        ins += [h, y_fwd, d_skip, gain]
        specs += [col(G_SZ), pl.BlockSpec((None, tb, GW), lambda b, n: (b, blk(n), 0)),
                  full(d_skip), full(gain)]
    scratch = [pltpu.VMEM((SSD_GROUPS, SSD_STATE, 2 * LANES), F32),
               pltpu.VMEM((tb + 2 * SUBLANES, GW), F32),
               pltpu.VMEM((tb, 2 * GW), F32),
               pltpu.VMEM((tb, LANES), F32), pltpu.VMEM((tb, LANES), F32)]
    if final:
        scratch.append(pltpu.VMEM((tb, GW), F32))
    kernel = functools.partial(_ssd_kernel, reverse=reverse, final=final, chunk=chunk, tb=tb, nb=nb)
    return pl.pallas_call(
        kernel,
        grid=(bsz, nb),
        in_specs=specs,
        out_specs=pl.BlockSpec((None, tb, GW), lambda b, n: (b, blk(n), 0)),
        out_shape=jax.ShapeDtypeStruct((bsz, seq, GW), BF16 if final else F32),
        scratch_shapes=scratch,
        compiler_params=_params(("arbitrary", "arbitrary")),
        name=f"ssd_{'bwd' if reverse else 'fwd'}",
    )(*ins)


def _out_kernel(ya_ref, yh_ref, ys_ref, yg_ref, x_ref, w_ref, g_ref, b_ref, xo_ref, xb_ref):
    acc = _dot(ya_ref[...], w_ref[0:GW, :])
    acc = acc + _dot(yh_ref[...], w_ref[GW:2 * GW, :])
    acc = acc + _dot(ys_ref[...], w_ref[2 * GW:3 * GW, :])
    acc = acc + _dot(yg_ref[...], w_ref[3 * GW:4 * GW, :])
    r = DEEPNORM_ALPHA * x_ref[...] + acc
    mu = jnp.mean(r, axis=-1, keepdims=True)
    rc = r - mu
    var = jnp.mean(rc * rc, axis=-1, keepdims=True)
    y = rc * lax.rsqrt(var + 1e-5) * g_ref[...] + b_ref[...]
    xo_ref[...] = y
    xb_ref[...] = y.astype(BF16)


def _out_proj_ln(ya, yh, ys, yg, x, w, ln_g, ln_b):
    t, d = x.shape
    tm = min(512, t)
    row = lambda width: pl.BlockSpec((tm, width), lambda i: (i, 0))
    const = lambda arr: pl.BlockSpec(arr.shape, lambda i: (0, 0))
    return pl.pallas_call(
        _out_kernel,
        grid=(t // tm,),
        in_specs=[row(GW), row(GW), row(GW), row(GW), row(d), const(w), const(ln_g), const(ln_b)],
        out_specs=[row(d), row(d)],
        out_shape=[jax.ShapeDtypeStruct((t, d), F32), jax.ShapeDtypeStruct((t, d), BF16)],
        compiler_params=_params(("parallel",)),
        name="out_proj_ln",
    )(ya, yh, ys, yg, x, w, ln_g, ln_b)


def _pad_heads(w, heads, width):
    d = w.shape[0]
    w = w.reshape(d, heads, width)
    w = jnp.pad(w, ((0, 0), (0, 0), (0, LANES - width)))
    return w.reshape(d, heads * LANES)


def _layout_w_in(w):
    d = w.shape[0]
    sizes = (512, 128, 128, 512, 512, 512, 512, 512, 512, 1024, 8, 8, 512, 256, 256, 512, 16, 16, 512)
    offs = np.concatenate([[0], np.cumsum(sizes)])
    (a_q, a_k, a_v, a_z, h_q, h_ff, h_fb, h_i, h_z,
     s_xbc, s_dtf, s_dtb, s_z, g_q, g_k, g_v, g_lf, g_lb, g_z) = [
        w[:, int(offs[i]):int(offs[i + 1])] for i in range(len(sizes))]
    zeros = lambda n: jnp.zeros((d, n), w.dtype)
    groups = [
        a_q, a_z, jnp.concatenate([a_k, a_v, zeros(256)], axis=1),
        h_q, h_ff, h_fb, h_i, h_z,
        s_xbc[:, :512], s_xbc[:, 512:], s_z,
        _pad_heads(g_q, GLA_HEADS, GLA_HEAD_K), _pad_heads(g_k, GLA_HEADS, GLA_HEAD_K), g_v, g_z,
        jnp.concatenate([s_dtf, s_dtb, zeros(112), g_lf, g_lb, zeros(96), zeros(256)], axis=1),
    ]
    return jnp.concatenate(groups, axis=1).astype(BF16)


def _rope_tables(seq):
    rows = seq // GRID_W
    row_pos = jnp.repeat(jnp.arange(rows, dtype=F32), GRID_W)
    col_pos = jnp.tile(jnp.arange(GRID_W, dtype=F32), rows)
    axis_dim = ATTN_HEAD_DIM // 2
    inv_freq = jnp.power(ROPE_THETA, -jnp.arange(0, axis_dim, 2, dtype=F32) / axis_dim)
    ang_r = row_pos[:, None] * inv_freq
    ang_c = col_pos[:, None] * inv_freq
    cos = jnp.concatenate([jnp.cos(ang_r)] * 2 + [jnp.cos(ang_c)] * 2, axis=1)
    sin = jnp.concatenate([-jnp.sin(ang_r), jnp.sin(ang_r), -jnp.sin(ang_c), jnp.sin(ang_c)], axis=1)
    return jnp.tile(cos, (1, 2)), jnp.tile(sin, (1, 2))


def _ssd_expand(reverse):
    m = np.zeros((LANES, GW), np.float32)
    for hd in range(SSD_HEADS):
        m[hd + (SSD_HEADS if reverse else 0), hd * SSD_HEAD_DIM:(hd + 1) * SSD_HEAD_DIM] = 1.0
    return jnp.asarray(m, BF16)


def _tri(chunk, reverse):
    t = np.arange(chunk)[:, None]
    u = np.arange(chunk)[None, :]
    return jnp.asarray((u >= t) if reverse else (u <= t), BF16)


def kernel(x, w_in, attn_q_norm, attn_k_norm, hgrn_lb_logits, hgrn_norm, ssd_conv_w, ssd_conv_b,
           ssd_dt_bias, ssd_a_log, ssd_d, ssd_norm, gla_gk_w2, gla_gk_b, gla_norm, w_out, ln_g, ln_b):
    bsz, seq, d = x.shape
    t = bsz * seq
    rope = _rope_tables(seq)
    blocksum = jnp.asarray(np.kron(np.eye(GW // ATTN_HEAD_DIM), np.ones((ATTN_HEAD_DIM, ATTN_HEAD_DIM))), BF16)
    lower_bounds = jnp.cumsum(jax.nn.softmax(hgrn_lb_logits.astype(F32), axis=0), axis=0)
    lower_bounds = lower_bounds - lower_bounds[0]
    scan_consts = {rev: _scan_constants(LIN_CHUNK, min(SCAN_PRE_ROWS, seq), rev) for rev in (False, True)}
    ssd_chunk = min(SSD_CHUNK, min(SCAN_TOKENS, seq))
    ssd_consts = {rev: (_ssd_expand(rev), _tri(ssd_chunk, rev)) for rev in (False, True)}

    xf = x.reshape(t, d)
    xb = xf.astype(BF16)
    for i in range(DEPTH):
        h = _in_proj(xb, _layout_w_in(w_in[i])).reshape(bsz, seq, N_PROJ)

        y_a = _attention(h, rope, rope, jnp.tile(attn_q_norm[i], ATTN_HEADS)[None, :].astype(F32),
                         jnp.tile(attn_k_norm[i], 2)[None, :].astype(F32), blocksum)

        lb = jnp.maximum(lower_bounds[i], 0.0)[None, :]
        hgrn_params = (jnp.log(lb), jnp.log1p(-lb), hgrn_norm[i][None, :].astype(F32))
        o_f = _gated_scan(h, "hgrn", False, scan_consts[False], hgrn_params)
        y_h = _gated_scan(h, "hgrn", True, scan_consts[True], hgrn_params, o_f)

        gla_out = None
        for rev in (False, True):
            w2 = jnp.zeros((LANES, GW), F32)
            w2 = w2.at[rev * GLA_GATE_RANK:(rev + 1) * GLA_GATE_RANK].set(
                _pad_heads(gla_gk_w2[i, int(rev)], GLA_HEADS, GLA_HEAD_K))
            gb = _pad_heads(gla_gk_b[i, int(rev)][None, :], GLA_HEADS, GLA_HEAD_K)
            gla_params = (w2.astype(BF16), gb.astype(F32), jnp.tile(gla_norm[i], GLA_HEADS)[None, :].astype(F32))
            gla_out = _gated_scan(h, "gla", rev, scan_consts[rev], gla_params, gla_out)
        y_g = gla_out

        ssd_out = None
        for rev in (False, True):
            dt_bias = jnp.zeros((1, LANES), F32).at[0, :2 * SSD_HEADS].set(ssd_dt_bias[i].reshape(-1))
            a_coef = jnp.zeros((1, LANES), F32).at[0, :2 * SSD_HEADS].set(-jnp.exp(ssd_a_log[i].astype(F32)).reshape(-1))
            expand, tri = ssd_consts[rev]
            ssd_params = (ssd_conv_w[i].astype(F32), ssd_conv_b[i][None, :].astype(F32), dt_bias, a_coef,
                          expand, tri, jnp.repeat(ssd_d[i].astype(F32), SSD_HEAD_DIM)[None, :],
                          ssd_norm[i][None, :].astype(F32))
            ssd_out = _ssd(h, rev, ssd_params, ssd_out)
        y_s = ssd_out

        flat = lambda a: a.reshape(t, GW)
        xf, xb = _out_proj_ln(flat(y_a), flat(y_h), flat(y_s), flat(y_g), xf, w_out[i].astype(BF16),
                              ln_g[i][None, :].astype(F32), ln_b[i][None, :].astype(F32))
    return xf.reshape(bsz, seq, d)
```

```python
import functools

import numpy as np
import jax
import jax.numpy as jnp
from jax import lax
from jax.experimental import pallas as pl
from jax.experimental.pallas import tpu as pltpu

F32 = jnp.float32
BF16 = jnp.bfloat16

D_MODEL = 1024
DEPTH = 2
GROUP_WIDTH = 512
GRID_W = 64
ATTN_HEADS = 8
ATTN_KV_HEADS = 2
ATTN_HEAD_DIM = 64
ROPE_THETA = 10000.0
HGRN_HEADS = 4
HGRN_EXPAND = 128
SSD_HEADS = 8
SSD_HEAD_DIM = 64
SSD_GROUPS = 2
SSD_STATE = 128
SSD_CONV_WIDTH = 5
GLA_HEADS = 4
GLA_HEAD_K = 64
GLA_HEAD_V = 128
GLA_GATE_RANK = 16
GLA_GATE_NORMALIZER = 16.0
DEEPNORM_ALPHA = (2 * DEPTH) ** 0.25

LANES = 128
SUBLANES = 8
HALO = 16
GW = GROUP_WIDTH

(G_AQ, G_AZ, G_AKV, G_HQ, G_HFF, G_HFB, G_HI, G_HZ,
 G_SX, G_SBC, G_SZ, G_GQ, G_GK, G_GV, G_GZ, G_MISC) = range(16)
N_GROUPS = 16
N_PROJ = N_GROUPS * GW

LIN_CHUNK = 64
LIN_SUB = 4
SSD_CHUNK = 128
SCAN_TOKENS = 512
SCAN_PRE_ROWS = 128
ATTN_TQ = 256
ATTN_KEY_BLOCKS = 2
VMEM_LIMIT = 48 * 1024 * 1024


def _dot(a, b):
    return jnp.dot(a, b, preferred_element_type=F32)


def _dot_nt(a, b):
    return lax.dot_general(a, b, (((1,), (1,)), ((), ())), preferred_element_type=F32)


def _dot_tn(a, b):
    return lax.dot_general(a, b, (((0,), (0,)), ((), ())), preferred_element_type=F32)


def _split3(x):
    hi = x.astype(BF16)
    r1 = x - hi.astype(F32)
    mid = r1.astype(BF16)
    lo = (r1 - mid.astype(F32)).astype(BF16)
    return hi, mid, lo


def _sel_dot(sel, x):
    hi, mid, lo = _split3(x)
    return _dot(sel, hi) + _dot(sel, mid) + _dot(sel, lo)


def _log_sigmoid(x):
    return jnp.minimum(x, 0.0) - jnp.log1p(jnp.exp(-jnp.abs(x)))


def _silu(x):
    return x * jax.nn.sigmoid(x)


def _softplus(x):
    return jnp.maximum(x, 0.0) + jnp.log1p(jnp.exp(-jnp.abs(x)))


def _params(sem):
    return pltpu.CompilerParams(dimension_semantics=sem, vmem_limit_bytes=VMEM_LIMIT)


def _matmul_kernel(x_ref, w_ref, o_ref):
    o_ref[...] = _dot(x_ref[...], w_ref[...]).astype(o_ref.dtype)


def _in_proj(xb, w):
    t, d = xb.shape
    n = w.shape[1]
    tm = min(1024, t)
    tn = GW
    return pl.pallas_call(
        _matmul_kernel,
        grid=(t // tm, n // tn),
        in_specs=[pl.BlockSpec((tm, d), lambda i, j: (i, 0)),
                  pl.BlockSpec((d, tn), lambda i, j: (0, j))],
        out_specs=pl.BlockSpec((tm, tn), lambda i, j: (i, j)),
        out_shape=jax.ShapeDtypeStruct((t, n), BF16),
        compiler_params=_params(("parallel", "parallel")),
        name="in_proj",
    )(xb, w)


def _head_rms(x, blocksum, gain, eps=1e-6):
    xx = x * x
    hi = xx.astype(BF16)
    lo = (xx - hi.astype(F32)).astype(BF16)
    ssq = _dot(hi, blocksum) + _dot(lo, blocksum)
    return x * lax.rsqrt(ssq * (1.0 / ATTN_HEAD_DIM) + eps) * gain


def _rope(x, cos, sin_signed):
    w = x.shape[1]
    lane = lax.broadcasted_iota(jnp.int32, x.shape, 1)
    first = (lane & 31) < 16
    partner = jnp.where(first, pltpu.roll(x, w - 16, 1), pltpu.roll(x, 16, 1))
    return x * cos + partner * sin_signed


def _attn_kernel(q_ref, z_ref, kv_ref, cos_ref, sin_ref, cosk_ref, sink_ref, qg_ref, kg_ref,
                 bs_ref, o_ref, kd_ref, vd_ref, *, seq, prep_rows):
    n = pl.program_id(1)

    @pl.when(n == 0)
    def _prep_kv():
        def body(i, carry):
            r0 = pl.multiple_of(i * prep_rows, prep_rows)
            kv = kv_ref[pl.ds(r0, prep_rows), :].astype(F32)
            k = kv[:, :LANES]
            v = kv[:, LANES:]
            k = _head_rms(k, bs_ref[0:LANES, 0:LANES], kg_ref[...])
            k = _rope(k, cosk_ref[pl.ds(r0, prep_rows), :], sink_ref[pl.ds(r0, prep_rows), :])
            lane = lax.broadcasted_iota(jnp.int32, k.shape, 1)
            low = lane < ATTN_HEAD_DIM
            k_sw = pltpu.roll(k, ATTN_HEAD_DIM, 1)
            v_sw = pltpu.roll(v, ATTN_HEAD_DIM, 1)
            kd_ref[0, pl.ds(r0, prep_rows), :] = jnp.where(low, k, k_sw).astype(BF16)
            kd_ref[1, pl.ds(r0, prep_rows), :] = jnp.where(low, k_sw, k).astype(BF16)
            vd_ref[0, pl.ds(r0, prep_rows), :] = jnp.where(low, v, v_sw).astype(BF16)
            vd_ref[1, pl.ds(r0, prep_rows), :] = jnp.where(low, v_sw, v).astype(BF16)
            return carry
        lax.fori_loop(0, seq // prep_rows, body, 0)

    q = _head_rms(q_ref[...].astype(F32), bs_ref[...], qg_ref[...])
    cos = jnp.concatenate([cos_ref[...]] * 4, axis=1)
    sin = jnp.concatenate([sin_ref[...]] * 4, axis=1)
    q = _rope(q, cos, sin) * (ATTN_HEAD_DIM ** -0.5)
    lane = lax.broadcasted_iota(jnp.int32, (q.shape[0], LANES), 1)
    low = lane < ATTN_HEAD_DIM
    group = ATTN_HEADS // ATTN_KV_HEADS
    kb_len = seq // ATTN_KEY_BLOCKS

    q_heads = []
    for hd in range(ATTN_HEADS):
        pair, e = divmod(hd, 2)
        qp = q[:, pair * LANES:(pair + 1) * LANES]
        q_heads.append(jnp.where(low if e == 0 else jnp.logical_not(low), qp, 0.0).astype(BF16))

    def scores(item):
        hd, kb = divmod(item, ATTN_KEY_BLOCKS)
        return _dot_nt(q_heads[hd], kd_ref[hd // group, kb * kb_len:(kb + 1) * kb_len, :])

    outs = []
    n_items = ATTN_HEADS * ATTN_KEY_BLOCKS
    s_next = scores(0)
    m_run = l_run = o_run = None
    for item in range(n_items):
        hd, kb = divmod(item, ATTN_KEY_BLOCKS)
        s = s_next
        if item + 1 < n_items:
            s_next = scores(item + 1)
        v_blk = vd_ref[hd // group, kb * kb_len:(kb + 1) * kb_len, :]
        m_blk = jnp.max(s, axis=-1, keepdims=True)
        if kb == 0:
            m_run = m_blk
            p = jnp.exp(s - m_run)
            l_run = jnp.sum(p, axis=-1, keepdims=True)
            o_run = _dot(p.astype(BF16), v_blk)
        else:
            m_new = jnp.maximum(m_run, m_blk)
            alpha = jnp.exp(m_run - m_new)
            p = jnp.exp(s - m_new)
            l_run = alpha * l_run + jnp.sum(p, axis=-1, keepdims=True)
            o_run = alpha * o_run + _dot(p.astype(BF16), v_blk)
            m_run = m_new
        if kb == ATTN_KEY_BLOCKS - 1:
            outs.append(o_run / l_run)
    for pair in range(ATTN_HEADS // 2):
        cols = slice(pair * LANES, (pair + 1) * LANES)
        o_pair = jnp.where(low, outs[2 * pair], outs[2 * pair + 1])
        o_ref[:, cols] = (o_pair * _silu(z_ref[:, cols].astype(F32))).astype(o_ref.dtype)


def _attention(h, rope, q_gain, k_gain, blocksum):
    bsz, seq, _ = h.shape
    tq = min(ATTN_TQ, seq)
    prep_rows = min(512, seq)
    cos2, sin2 = rope
    kernel = functools.partial(_attn_kernel, seq=seq, prep_rows=prep_rows)
    return pl.pallas_call(
        kernel,
        grid=(bsz, seq // tq),
        in_specs=[
            pl.BlockSpec((None, tq, GW), lambda b, n: (b, n, G_AQ)),
            pl.BlockSpec((None, tq, GW), lambda b, n: (b, n, G_AZ)),
            pl.BlockSpec((None, seq, 2 * LANES), lambda b, n: (b, 0, G_AKV * 2)),
            pl.BlockSpec((tq, LANES), lambda b, n: (n, 0)),
            pl.BlockSpec((tq, LANES), lambda b, n: (n, 0)),
            pl.BlockSpec((seq, LANES), lambda b, n: (0, 0)),
            pl.BlockSpec((seq, LANES), lambda b, n: (0, 0)),
            pl.BlockSpec((1, GW), lambda b, n: (0, 0)),
            pl.BlockSpec((1, LANES), lambda b, n: (0, 0)),
            pl.BlockSpec((GW, GW), lambda b, n: (0, 0)),
        ],
        out_specs=pl.BlockSpec((None, tq, GW), lambda b, n: (b, n, 0)),
        out_shape=jax.ShapeDtypeStruct((bsz, seq, GW), BF16),
        scratch_shapes=[pltpu.VMEM((2, seq, LANES), BF16), pltpu.VMEM((2, seq, LANES), BF16)],
        compiler_params=_params(("arbitrary", "arbitrary")),
        name="attention",
    )(h, h, h, cos2, sin2, cos2, sin2, q_gain, k_gain, blocksum)


def _scan_constants(chunk, tb, reverse):
    t = np.arange(chunk)[:, None]
    u = np.arange(chunk)[None, :]
    tri = (u >= t) if reverse else (u <= t)
    q_half = 0 if reverse else 1
    masks = []
    levels = []
    h = chunk // 2
    while h >= LIN_SUB:
        same_block = (t // (2 * h)) == (u // (2 * h))
        masks.append(same_block & ((t // h) % 2 == q_half) & ((u // h) % 2 == 1 - q_half))
        levels.append((h, q_half))
        h //= 2
    masks = np.stack(masks, axis=0).astype(np.float32)
    tri_blocks = np.kron(np.eye(tb // chunk), tri)
    return jnp.asarray(tri_blocks, BF16), jnp.asarray(masks, F32), tuple(levels)


def _scan_kernel(*refs, mode, reverse, final, chunk, tb, levels, pre_rows):
    it = iter(refs)
    if mode == "hgrn":
        q_ref, f_ref, v_ref = next(it), next(it), next(it)
        loglb_ref, log1m_ref = next(it), next(it)
    else:
        q_ref, k_ref, v_ref, low_ref = next(it), next(it), next(it), next(it)
        w2_ref, gb_ref = next(it), next(it)
    if final:
        z_ref, of_ref, gain_ref = next(it), next(it), next(it)
    stack_ref, lmask_ref = next(it), next(it)
    o_ref = next(it)
    st_ref, qs_ref, ks_ref, gs_ref, bs_ref = (next(it) for _ in range(5))
    if final:
        os_ref = next(it)

    n = pl.program_id(1)

    @pl.when(n == 0)
    def _reset():
        st_ref[...] = jnp.zeros_like(st_ref)

    if mode == "gla":
        gs_ref[...] = _dot(low_ref[...], w2_ref[...]) + gb_ref[...]
    for r in range(0, tb, pre_rows):
        rows = slice(r, r + pre_rows)
        if mode == "hgrn":
            qr = q_ref[rows, :].astype(F32)
            qs_ref[rows, :] = _silu(qr) * (HGRN_EXPAND ** -0.5)
            f = f_ref[rows, :].astype(F32)
            ls = _log_sigmoid(f)
            a = loglb_ref[...]
            c = log1m_ref[...] + ls
            g_rows = jnp.maximum(a, c) + jnp.log1p(jnp.exp(-jnp.abs(a - c)))
            ks_ref[rows, :] = jnp.exp(log1m_ref[...] + ls - f)
        else:
            qs_ref[rows, :] = q_ref[rows, :].astype(F32) * (GLA_HEAD_K ** -0.5)
            ks_ref[rows, :] = k_ref[rows, :].astype(F32)
            g_rows = _log_sigmoid(gs_ref[rows, :]) * (1.0 / GLA_GATE_NORMALIZER)
        gs_ref[rows, :] = g_rows
        bs_ref[rows, :] = _sel_dot(stack_ref[...], g_rows)

    nc = tb // chunk
    shift = (chunk - 1) if reverse else 1
    row = lax.broadcasted_iota(jnp.int32, (chunk, LANES), 0)
    rr = lax.broadcasted_iota(jnp.int32, (chunk, chunk), 0)
    cc = lax.broadcasted_iota(jnp.int32, (chunk, chunk), 1)
    diff = (cc - rr) if reverse else (rr - cc)
    sub_edge = (LIN_SUB - 1) if reverse else 0
    keep = (row & (LIN_SUB - 1)) != sub_edge
    end_row = 0 if reverse else chunk - 1

    def body(ci, carry):
        c = (nc - 1 - ci) if reverse else ci
        r0 = pl.multiple_of(c * chunk, chunk)
        heads = range(4)
        lanes = [slice(hh * LANES, (hh + 1) * LANES) for hh in heads]
        q = [qs_ref[pl.ds(r0, chunk), ln] for ln in lanes]
        k = [ks_ref[pl.ds(r0, chunk), ln] for ln in lanes]
        g = [gs_ref[pl.ds(r0, chunk), ln] for ln in lanes]
        v = [v_ref[pl.ds(r0, chunk), ln] for ln in lanes]
        b = [bs_ref[pl.ds(r0, chunk), ln] for ln in lanes]

        o = []
        for hh in heads:
            st = st_ref[hh]
            total = b[hh][end_row:end_row + 1, :]
            o.append(_dot_nt((q[hh] * jnp.exp(b[hh])).astype(BF16), st.astype(BF16)))
            k_end = (k[hh] * jnp.exp(total - b[hh])).astype(BF16)
            st_ref[hh] = jnp.exp(total) * st + _dot_tn(v[hh], k_end)

        scores = [jnp.zeros((chunk, chunk), F32) for _ in heads]
        for li, (h, q_half) in enumerate(levels):
            is_q = ((row // h) & 1) == q_half
            ref_row = h - 1 if reverse else h
            for hh in heads:
                b3 = b[hh].reshape(chunk // (2 * h), 2 * h, LANES)
                d3 = b3 - b3[:, ref_row:ref_row + 1, :]
                log_e = -jnp.abs(d3.reshape(chunk, LANES))
                x = (jnp.where(is_q, q[hh], k[hh]) * jnp.exp(log_e)).astype(BF16)
                scores[hh] = scores[hh] + _dot_nt(x, x) * lmask_ref[li]

        for hh in heads:
            f0 = jnp.where(keep, jnp.exp(g[hh]), 0.0)
            a = k[hh]
            band = jnp.where(diff == 0, jnp.sum(q[hh] * a, axis=-1, keepdims=True), 0.0)
            for d in range(1, LIN_SUB):
                a = f0 * pltpu.roll(a, shift, 0)
                band = jnp.where(diff == d, jnp.sum(q[hh] * a, axis=-1, keepdims=True), band)
            scores[hh] = scores[hh] + band

        for hh in heads:
            out = o[hh] + _dot(scores[hh].astype(BF16), v[hh])
            if final:
                os_ref[pl.ds(r0, chunk), lanes[hh]] = out
            else:
                o_ref[pl.ds(r0, chunk), lanes[hh]] = out
        return carry

    lax.fori_loop(0, nc, body, 0, unroll=2)

    if final:
        o = os_ref[...] + of_ref[...]
        if mode == "hgrn":
            ms = jnp.mean(o * o, axis=-1, keepdims=True)
            y = o * lax.rsqrt(ms + 1e-6) * gain_ref[...]
        else:
            parts = []
            for hh in range(4):
                oh = o[:, hh * LANES:(hh + 1) * LANES]
                ms = jnp.mean(oh * oh, axis=-1, keepdims=True)
                parts.append(oh * lax.rsqrt(ms + 1e-6))
            y = jnp.concatenate(parts, axis=1) * gain_ref[...]
        o_ref[...] = (y * _silu(z_ref[...].astype(F32))).astype(o_ref.dtype)


def _gated_scan(h, mode, reverse, consts, params, o_fwd=None):
    bsz, seq, _ = h.shape
    tb = min(SCAN_TOKENS, seq)
    nb = seq // tb
    chunk = LIN_CHUNK
    stack, lmask, levels = consts
    final = reverse

    def blk(n):
        return (nb - 1 - n) if reverse else n

    def col(group):
        return pl.BlockSpec((None, tb, GW), lambda b, n: (b, blk(n), group))

    def full(arr):
        nd = arr.ndim
        return pl.BlockSpec(arr.shape, lambda b, n: (0,) * nd)

    if mode == "hgrn":
        log_lb, log_1m_lb, gain = params
        ins = [h, h, h, log_lb, log_1m_lb]
        specs = [col(G_HQ), col(G_HFB if reverse else G_HFF), col(G_HI), full(log_lb), full(log_1m_lb)]
        z_group = G_HZ
    else:
        w2, gb, gain = params
        ins = [h, h, h, h, w2, gb]
        specs = [col(G_GQ), col(G_GK), col(G_GV),
                 pl.BlockSpec((None, tb, LANES), lambda b, n: (b, blk(n), G_MISC * 4 + 1)),
                 full(w2), full(gb)]
        z_group = G_GZ
    if final:
        ins += [h, o_fwd, gain]
        specs += [col(z_group), pl.BlockSpec((None, tb, GW), lambda b, n: (b, blk(n), 0)), full(gain)]
    ins += [stack, lmask]
    specs += [full(stack), full(lmask)]
    scratch = [pltpu.VMEM((4, LANES, LANES), F32)] + [pltpu.VMEM((tb, GW), F32)] * 4
    if final:
        scratch.append(pltpu.VMEM((tb, GW), F32))
    kernel = functools.partial(_scan_kernel, mode=mode, reverse=reverse, final=final,
                               chunk=chunk, tb=tb, levels=levels, pre_rows=stack.shape[0])
    return pl.pallas_call(
        kernel,
        grid=(bsz, nb),
        in_specs=specs,
        out_specs=pl.BlockSpec((None, tb, GW), lambda b, n: (b, blk(n), 0)),
        out_shape=jax.ShapeDtypeStruct((bsz, seq, GW), BF16 if final else F32),
        scratch_shapes=scratch,
        compiler_params=_params(("arbitrary", "arbitrary")),
        name=f"{mode}_{'bwd' if reverse else 'fwd'}",
    )(*ins)


def _ssd_kernel(*refs, reverse, chunk, tb, nb):
    it = iter(refs)
    if not reverse:
        x_ref, xp_ref, xn_ref, bc_ref, bcp_ref, bcn_ref, cw_ref, cb_ref = (next(it) for _ in range(8))
    else:
        ux_ref, ubc_ref = next(it), next(it)
    dt_ref, dtb_ref, acoef_ref, tri_ref, expand_ref = (next(it) for _ in range(5))
    if reverse:
        z_ref, yf_ref, dskip_ref, gain_ref = (next(it) for _ in range(4))
        o_ref = next(it)
    else:
        o_ref, ux_ref, ubc_ref = next(it), next(it), next(it)
    st_ref, dts_ref, as_ref = next(it), next(it), next(it)
    if reverse:
        ys_ref = next(it)
    else:
        ext_ref = next(it)

    n = pl.program_id(1)
    pos = (nb - 1 - n) if reverse else n

    @pl.when(n == 0)
    def _reset():
        st_ref[...] = jnp.zeros_like(st_ref)

    if not reverse:
        halves = ((x_ref, xp_ref, xn_ref, ux_ref), (bc_ref, bcp_ref, bcn_ref, ubc_ref))
        for hi, (cur, prev, nxt, dst) in enumerate(halves):
            cols = slice(hi * GW, (hi + 1) * GW)
            ext_ref[0:HALO, :] = jnp.where(pos > 0, prev[...].astype(F32), 0.0)
            ext_ref[HALO:HALO + tb, :] = cur[...].astype(F32)
            ext_ref[HALO + tb:2 * HALO + tb, :] = jnp.where(pos < nb - 1, nxt[...].astype(F32), 0.0)
            acc = jnp.broadcast_to(cb_ref[:, cols], (tb, GW))
            for tap in range(SSD_CONV_WIDTH):
                acc = acc + cw_ref[tap:tap + 1, cols] * ext_ref[HALO - 2 + tap:HALO - 2 + tap + tb, :]
            dst[...] = _silu(acc).astype(dst.dtype)

    dt = _softplus(dt_ref[...].astype(F32) + dtb_ref[...])
    dts_ref[...] = dt
    as_ref[...] = dt * acoef_ref[...]

    nc = tb // chunk
    head0 = SSD_HEADS if reverse else 0
    rr = lax.broadcasted_iota(jnp.int32, (chunk, chunk), 0)
    cc = lax.broadcasted_iota(jnp.int32, (chunk, chunk), 1)
    valid = (rr <= cc) if reverse else (rr >= cc)
    lane = lax.broadcasted_iota(jnp.int32, (chunk, LANES), 1)
    low = lane < SSD_HEAD_DIM
    end_row = 0 if reverse else chunk - 1

    def body(ci, carry):
        c = (nc - 1 - ci) if reverse else ci
        r0 = pl.multiple_of(c * chunk, chunk)
        rows = pl.ds(r0, chunk)
        dtc = dts_ref[rows, :]
        acs = _sel_dot(tri_ref[...], as_ref[rows, :])
        a_end = acs[end_row:end_row + 1, :]
        acs_t = jnp.transpose(acs)
        narrow = jnp.concatenate([dtc, jnp.exp(acs), jnp.exp(a_end - acs)], axis=0)
        hi = narrow.astype(BF16)
        lo = (narrow - hi.astype(F32)).astype(BF16)
        wide = _dot(hi, expand_ref[...]) + _dot(lo, expand_ref[...])
        dt_x = wide[0:chunk]
        e_acs = wide[chunk:2 * chunk]
        e_rest = wide[2 * chunk:3 * chunk]

        for g in range(SSD_GROUPS):
            bg = ubc_ref[rows, g * LANES:(g + 1) * LANES]
            cg = ubc_ref[rows, (SSD_GROUPS + g) * LANES:(SSD_GROUPS + g + 1) * LANES]
            gm = _dot_nt(cg, bg)
            st = st_ref[g]
            c_st = _dot(cg, st.astype(BF16))
            wx, e_end = [], []
            for jp in range(2):
                pair = 2 * g + jp
                pcols = slice(pair * LANES, (pair + 1) * LANES)
                ea = e_acs[:, pcols]
                xdt = ux_ref[rows, pcols] * dt_x[:, pcols]
                wx.append((e_rest[:, pcols] * xdt).astype(BF16))
                e_end.append(ea[end_row:end_row + 1, :])
                acc = c_st[:, jp * LANES:(jp + 1) * LANES] * ea
                for e in range(2):
                    hl = head0 + 2 * pair + e
                    dmat = acs[:, hl:hl + 1] - acs_t[hl:hl + 1, :]
                    lm = jnp.where(valid, jnp.exp(jnp.minimum(dmat, 0.0)), 0.0)
                    m = (gm * lm).astype(BF16)
                    xm = jnp.where(low if e == 0 else jnp.logical_not(low), xdt, 0.0).astype(BF16)
                    acc = acc + _dot(m, xm)
                if reverse:
                    ys_ref[rows, pcols] = acc
                else:
                    o_ref[rows, pcols] = acc
            st_ref[g] = (jnp.concatenate(e_end, axis=1) * st
                         + _dot_tn(bg, jnp.concatenate(wx, axis=1)))
        return carry

    lax.fori_loop(0, nc, body, 0)

    if reverse:
        y = ys_ref[...] + yf_ref[...] + dskip_ref[...] * ux_ref[...]
        y = y * _silu(z_ref[...].astype(F32))
        ms = jnp.mean(y * y, axis=-1, keepdims=True)
        o_ref[...] = (y * lax.rsqrt(ms + 1e-6) * gain_ref[...]).astype(o_ref.dtype)


def _ssd(h, reverse, params, fwd=None):
    bsz, seq, _ = h.shape
    tb = min(SCAN_TOKENS, seq)
    nb = seq // tb
    chunk = min(SSD_CHUNK, tb)
    conv_w, conv_b, dt_bias, a_coef, tri, expand, d_skip, gain = params
    hb = tb // HALO
    nhb = seq // HALO

    def blk(n):
        return (nb - 1 - n) if reverse else n

    def col(group, width=GW, scale=1):
        return pl.BlockSpec((None, tb, width), lambda b, n: (b, blk(n), group * scale))

    def prev(group):
        return pl.BlockSpec((None, HALO, GW), lambda b, n: (b, jnp.maximum(blk(n) * hb - 1, 0), group))

    def nxt(group):
        return pl.BlockSpec((None, HALO, GW), lambda b, n: (b, jnp.minimum((blk(n) + 1) * hb, nhb - 1), group))

    def full(arr):
        nd = arr.ndim
        return pl.BlockSpec(arr.shape, lambda b, n: (0,) * nd)

    plain = pl.BlockSpec((None, tb, GW), lambda b, n: (b, blk(n), 0))
    dt_spec = pl.BlockSpec((None, tb, LANES), lambda b, n: (b, blk(n), G_MISC * 4))
    state = pltpu.VMEM((SSD_GROUPS, SSD_STATE, 2 * LANES), F32)
    narrow = pltpu.VMEM((tb, LANES), F32)
    kernel = functools.partial(_ssd_kernel, reverse=reverse, chunk=chunk, tb=tb, nb=nb)
    if not reverse:
        return pl.pallas_call(
            kernel,
            grid=(bsz, nb),
            in_specs=[col(G_SX), prev(G_SX), nxt(G_SX), col(G_SBC), prev(G_SBC), nxt(G_SBC),
                      full(conv_w), full(conv_b), dt_spec, full(dt_bias), full(a_coef), full(tri),
                      full(expand)],
            out_specs=[plain, plain, plain],
            out_shape=[jax.ShapeDtypeStruct((bsz, seq, GW), F32),
                       jax.ShapeDtypeStruct((bsz, seq, GW), F32),
                       jax.ShapeDtypeStruct((bsz, seq, GW), BF16)],
            scratch_shapes=[state, narrow, narrow, pltpu.VMEM((tb + 2 * HALO, GW), F32)],
            compiler_params=_params(("arbitrary", "arbitrary")),
            name="ssd_fwd",
        )(h, h, h, h, h, h, conv_w, conv_b, h, dt_bias, a_coef, tri, expand)
    y_fwd, u_x, u_bc = fwd
    return pl.pallas_call(
        kernel,
        grid=(bsz, nb),
        in_specs=[plain, plain, dt_spec, full(dt_bias), full(a_coef), full(tri), full(expand),
                  col(G_SZ), plain, full(d_skip), full(gain)],
        out_specs=plain,
        out_shape=jax.ShapeDtypeStruct((bsz, seq, GW), BF16),
        scratch_shapes=[state, narrow, narrow, pltpu.VMEM((tb, GW), F32)],
        compiler_params=_params(("arbitrary", "arbitrary")),
        name="ssd_bwd",
    )(u_x, u_bc, h, dt_bias, a_coef, tri, expand, h, y_fwd, d_skip, gain)


def _out_kernel(ya_ref, yh_ref, ys_ref, yg_ref, x_ref, w_ref, g_ref, b_ref, xo_ref, xb_ref):
    acc = _dot(ya_ref[...], w_ref[0:GW, :])
    acc = acc + _dot(yh_ref[...], w_ref[GW:2 * GW, :])
    acc = acc + _dot(ys_ref[...], w_ref[2 * GW:3 * GW, :])
    acc = acc + _dot(yg_ref[...], w_ref[3 * GW:4 * GW, :])
    r = DEEPNORM_ALPHA * x_ref[...] + acc
    mu = jnp.mean(r, axis=-1, keepdims=True)
    rc = r - mu
    var = jnp.mean(rc * rc, axis=-1, keepdims=True)
    y = rc * lax.rsqrt(var + 1e-5) * g_ref[...] + b_ref[...]
    xo_ref[...] = y
    xb_ref[...] = y.astype(BF16)


def _out_proj_ln(ya, yh, ys, yg, x, w, ln_g, ln_b):
    t, d = x.shape
    tm = min(512, t)
    row = lambda width: pl.BlockSpec((tm, width), lambda i: (i, 0))
    const = lambda arr: pl.BlockSpec(arr.shape, lambda i: (0, 0))
    return pl.pallas_call(
        _out_kernel,
        grid=(t // tm,),
        in_specs=[row(GW), row(GW), row(GW), row(GW), row(d), const(w), const(ln_g), const(ln_b)],
        out_specs=[row(d), row(d)],
        out_shape=[jax.ShapeDtypeStruct((t, d), F32), jax.ShapeDtypeStruct((t, d), BF16)],
        compiler_params=_params(("parallel",)),
        name="out_proj_ln",
    )(ya, yh, ys, yg, x, w, ln_g, ln_b)


def _pad_heads(w, heads, width):
    d = w.shape[0]
    w = w.reshape(d, heads, width)
    w = jnp.pad(w, ((0, 0), (0, 0), (0, LANES - width)))
    return w.reshape(d, heads * LANES)


def _layout_w_in(w):
    d = w.shape[0]
    sizes = (512, 128, 128, 512, 512, 512, 512, 512, 512, 1024, 8, 8, 512, 256, 256, 512, 16, 16, 512)
    offs = np.concatenate([[0], np.cumsum(sizes)])
    (a_q, a_k, a_v, a_z, h_q, h_ff, h_fb, h_i, h_z,
     s_xbc, s_dtf, s_dtb, s_z, g_q, g_k, g_v, g_lf, g_lb, g_z) = [
        w[:, int(offs[i]):int(offs[i + 1])] for i in range(len(sizes))]
    zeros = lambda n: jnp.zeros((d, n), w.dtype)
    groups = [
        a_q, a_z, jnp.concatenate([a_k, a_v, zeros(256)], axis=1),
        h_q, h_ff, h_fb, h_i, h_z,
        s_xbc[:, :512], s_xbc[:, 512:], s_z,
        _pad_heads(g_q, GLA_HEADS, GLA_HEAD_K), _pad_heads(g_k, GLA_HEADS, GLA_HEAD_K), g_v, g_z,
        jnp.concatenate([s_dtf, s_dtb, zeros(112), g_lf, g_lb, zeros(96), zeros(256)], axis=1),
    ]
    return jnp.concatenate(groups, axis=1).astype(BF16)


def _rope_tables(seq):
    rows = seq // GRID_W
    row_pos = jnp.repeat(jnp.arange(rows, dtype=F32), GRID_W)
    col_pos = jnp.tile(jnp.arange(GRID_W, dtype=F32), rows)
    axis_dim = ATTN_HEAD_DIM // 2
    inv_freq = jnp.power(ROPE_THETA, -jnp.arange(0, axis_dim, 2, dtype=F32) / axis_dim)
    ang_r = row_pos[:, None] * inv_freq
    ang_c = col_pos[:, None] * inv_freq
    cos = jnp.concatenate([jnp.cos(ang_r)] * 2 + [jnp.cos(ang_c)] * 2, axis=1)
    sin = jnp.concatenate([-jnp.sin(ang_r), jnp.sin(ang_r), -jnp.sin(ang_c), jnp.sin(ang_c)], axis=1)
    return jnp.tile(cos, (1, 2)), jnp.tile(sin, (1, 2))


def _ssd_expand(reverse):
    m = np.zeros((LANES, GW), np.float32)
    for hd in range(SSD_HEADS):
        m[hd + (SSD_HEADS if reverse else 0), hd * SSD_HEAD_DIM:(hd + 1) * SSD_HEAD_DIM] = 1.0
    return jnp.asarray(m, BF16)


def _tri(chunk, reverse):
    t = np.arange(chunk)[:, None]
    u = np.arange(chunk)[None, :]
    return jnp.asarray((u >= t) if reverse else (u <= t), BF16)


def kernel(x, w_in, attn_q_norm, attn_k_norm, hgrn_lb_logits, hgrn_norm, ssd_conv_w, ssd_conv_b,
           ssd_dt_bias, ssd_a_log, ssd_d, ssd_norm, gla_gk_w2, gla_gk_b, gla_norm, w_out, ln_g, ln_b):
    bsz, seq, d = x.shape
    t = bsz * seq
    rope = _rope_tables(seq)
    blocksum = jnp.asarray(np.kron(np.eye(GW // ATTN_HEAD_DIM), np.ones((ATTN_HEAD_DIM, ATTN_HEAD_DIM))), BF16)
    lower_bounds = jnp.cumsum(jax.nn.softmax(hgrn_lb_logits.astype(F32), axis=0), axis=0)
    lower_bounds = lower_bounds - lower_bounds[0]
    scan_consts = {rev: _scan_constants(LIN_CHUNK, min(SCAN_PRE_ROWS, seq), rev) for rev in (False, True)}
    ssd_chunk = min(SSD_CHUNK, min(SCAN_TOKENS, seq))

    xf = x.reshape(t, d)
    xb = xf.astype(BF16)
    for i in range(DEPTH):
        h = _in_proj(xb, _layout_w_in(w_in[i])).reshape(bsz, seq, N_PROJ)

        y_a = _attention(h, rope, jnp.tile(attn_q_norm[i], ATTN_HEADS)[None, :].astype(F32),
                         jnp.tile(attn_k_norm[i], 2)[None, :].astype(F32), blocksum)

        lb = jnp.maximum(lower_bounds[i], 0.0)[None, :]
        hgrn_params = (jnp.log(lb), jnp.log1p(-lb), hgrn_norm[i][None, :].astype(F32))
        o_f = _gated_scan(h, "hgrn", False, scan_consts[False], hgrn_params)
        y_h = _gated_scan(h, "hgrn", True, scan_consts[True], hgrn_params, o_f)

        gla_out = None
        for rev in (False, True):
            w2 = jnp.zeros((LANES, GW), F32)
            w2 = w2.at[rev * GLA_GATE_RANK:(rev + 1) * GLA_GATE_RANK].set(
                _pad_heads(gla_gk_w2[i, int(rev)], GLA_HEADS, GLA_HEAD_K))
            gb = _pad_heads(gla_gk_b[i, int(rev)][None, :], GLA_HEADS, GLA_HEAD_K)
            gla_params = (w2.astype(BF16), gb.astype(F32), jnp.tile(gla_norm[i], GLA_HEADS)[None, :].astype(F32))
            gla_out = _gated_scan(h, "gla", rev, scan_consts[rev], gla_params, gla_out)
        y_g = gla_out

        dt_bias = jnp.zeros((1, LANES), F32).at[0, :2 * SSD_HEADS].set(ssd_dt_bias[i].astype(F32).reshape(-1))
        a_coef = jnp.zeros((1, LANES), F32).at[0, :2 * SSD_HEADS].set(-jnp.exp(ssd_a_log[i].astype(F32)).reshape(-1))
        ssd_out = None
        for rev in (False, True):
            ssd_params = (ssd_conv_w[i].astype(F32), ssd_conv_b[i][None, :].astype(F32), dt_bias, a_coef,
                          _tri(ssd_chunk, rev), _ssd_expand(rev), jnp.repeat(ssd_d[i].astype(F32), SSD_HEAD_DIM)[None, :],
                          ssd_norm[i][None, :].astype(F32))
            ssd_out = _ssd(h, rev, ssd_params, ssd_out)
        y_s = ssd_out

        flat = lambda a: a.reshape(t, GW)
        xf, xb = _out_proj_ln(flat(y_a), flat(y_h), flat(y_s), flat(y_g), xf, w_out[i].astype(BF16),
                              ln_g[i][None, :].astype(F32), ln_b[i][None, :].astype(F32))
    return xf.reshape(bsz, seq, d)
```

```python
import functools

import numpy as np
import jax
import jax.numpy as jnp
from jax import lax
from jax.experimental import pallas as pl
from jax.experimental.pallas import tpu as pltpu

F32 = jnp.float32
BF16 = jnp.bfloat16

D_MODEL = 1024
DEPTH = 2
GROUP_WIDTH = 512
GRID_W = 64
ATTN_HEADS = 8
ATTN_KV_HEADS = 2
ATTN_HEAD_DIM = 64
ROPE_THETA = 10000.0
HGRN_HEADS = 4
HGRN_EXPAND = 128
SSD_HEADS = 8
SSD_HEAD_DIM = 64
SSD_GROUPS = 2
SSD_STATE = 128
SSD_CONV_WIDTH = 5
GLA_HEADS = 4
GLA_HEAD_K = 64
GLA_HEAD_V = 128
GLA_GATE_RANK = 16
GLA_GATE_NORMALIZER = 16.0
DEEPNORM_ALPHA = (2 * DEPTH) ** 0.25

LANES = 128
SUBLANES = 8
HALO = 16
GW = GROUP_WIDTH

(G_AQ, G_AZ, G_AKV, G_HQ, G_HFF, G_HFB, G_HI, G_HZ,
 G_SX, G_SBC, G_SZ, G_GQ, G_GK, G_GV, G_GZ, G_MISC) = range(16)
N_GROUPS = 16
N_PROJ = N_GROUPS * GW

LIN_CHUNK = 64
LIN_SUB = 4
SSD_CHUNK = 128
SCAN_TOKENS = 512
SCAN_PRE_ROWS = 128
ATTN_TQ = 256
ATTN_KEY_BLOCKS = 2
VMEM_LIMIT = 48 * 1024 * 1024


def _dot(a, b):
    return jnp.dot(a, b, preferred_element_type=F32)


def _dot_nt(a, b):
    return lax.dot_general(a, b, (((1,), (1,)), ((), ())), preferred_element_type=F32)


def _dot_tn(a, b):
    return lax.dot_general(a, b, (((0,), (0,)), ((), ())), preferred_element_type=F32)


def _split3(x):
    hi = x.astype(BF16)
    r1 = x - hi.astype(F32)
    mid = r1.astype(BF16)
    lo = (r1 - mid.astype(F32)).astype(BF16)
    return hi, mid, lo


def _sel_dot(sel, x):
    hi, mid, lo = _split3(x)
    return _dot(sel, hi) + _dot(sel, mid) + _dot(sel, lo)


def _log_sigmoid(x):
    return jnp.minimum(x, 0.0) - jnp.log1p(jnp.exp(-jnp.abs(x)))


def _silu(x):
    return x * jax.nn.sigmoid(x)


def _softplus(x):
    return jnp.maximum(x, 0.0) + jnp.log1p(jnp.exp(-jnp.abs(x)))


def _params(sem):
    return pltpu.CompilerParams(dimension_semantics=sem, vmem_limit_bytes=VMEM_LIMIT)


def _matmul_kernel(x_ref, w_ref, o_ref):
    o_ref[...] = _dot(x_ref[...], w_ref[...]).astype(o_ref.dtype)


def _in_proj(xb, w):
    t, d = xb.shape
    n = w.shape[1]
    tm = min(2048, t)
    tn = GW
    return pl.pallas_call(
        _matmul_kernel,
        grid=(t // tm, n // tn),
        in_specs=[pl.BlockSpec((tm, d), lambda i, j: (i, 0)),
                  pl.BlockSpec((d, tn), lambda i, j: (0, j))],
        out_specs=pl.BlockSpec((tm, tn), lambda i, j: (i, j)),
        out_shape=jax.ShapeDtypeStruct((t, n), BF16),
        compiler_params=_params(("parallel", "parallel")),
        name="in_proj",
    )(xb, w)


def _head_rms(x, blocksum, gain, eps=1e-6):
    xx = x * x
    hi = xx.astype(BF16)
    lo = (xx - hi.astype(F32)).astype(BF16)
    ssq = _dot(hi, blocksum) + _dot(lo, blocksum)
    return x * lax.rsqrt(ssq * (1.0 / ATTN_HEAD_DIM) + eps) * gain


def _rope(x, cos, sin_signed):
    w = x.shape[1]
    lane = lax.broadcasted_iota(jnp.int32, x.shape, 1)
    first = (lane & 31) < 16
    partner = jnp.where(first, pltpu.roll(x, w - 16, 1), pltpu.roll(x, 16, 1))
    return x * cos + partner * sin_signed


def _attn_kernel(q_ref, z_ref, kv_ref, cos_ref, sin_ref, cosk_ref, sink_ref, qg_ref, kg_ref,
                 bs_ref, o_ref, kd_ref, vd_ref, *, seq, prep_rows):
    n = pl.program_id(1)

    @pl.when(n == 0)
    def _prep_kv():
        def body(i, carry):
            r0 = pl.multiple_of(i * prep_rows, prep_rows)
            kv = kv_ref[pl.ds(r0, prep_rows), :].astype(F32)
            k = kv[:, :LANES]
            v = kv[:, LANES:]
            k = _head_rms(k, bs_ref[0:LANES, 0:LANES], kg_ref[...])
            k = _rope(k, cosk_ref[pl.ds(r0, prep_rows), :], sink_ref[pl.ds(r0, prep_rows), :])
            lane = lax.broadcasted_iota(jnp.int32, k.shape, 1)
            low = lane < ATTN_HEAD_DIM
            k_sw = pltpu.roll(k, ATTN_HEAD_DIM, 1)
            v_sw = pltpu.roll(v, ATTN_HEAD_DIM, 1)
            kd_ref[0, pl.ds(r0, prep_rows), :] = jnp.where(low, k, k_sw).astype(BF16)
            kd_ref[1, pl.ds(r0, prep_rows), :] = jnp.where(low, k_sw, k).astype(BF16)
            vd_ref[0, pl.ds(r0, prep_rows), :] = jnp.where(low, v, v_sw).astype(BF16)
            vd_ref[1, pl.ds(r0, prep_rows), :] = jnp.where(low, v_sw, v).astype(BF16)
            return carry
        lax.fori_loop(0, seq // prep_rows, body, 0)

    q = _head_rms(q_ref[...].astype(F32), bs_ref[...], qg_ref[...])
    cos = jnp.concatenate([cos_ref[...]] * 4, axis=1)
    sin = jnp.concatenate([sin_ref[...]] * 4, axis=1)
    q = _rope(q, cos, sin) * (ATTN_HEAD_DIM ** -0.5)
    lane = lax.broadcasted_iota(jnp.int32, (q.shape[0], LANES), 1)
    low = lane < ATTN_HEAD_DIM
    group = ATTN_HEADS // ATTN_KV_HEADS
    kb_len = seq // ATTN_KEY_BLOCKS

    q_heads = []
    for hd in range(ATTN_HEADS):
        pair, e = divmod(hd, 2)
        qp = q[:, pair * LANES:(pair + 1) * LANES]
        q_heads.append(jnp.where(low if e == 0 else jnp.logical_not(low), qp, 0.0).astype(BF16))

    def scores(item):
        hd, kb = divmod(item, ATTN_KEY_BLOCKS)
        return _dot_nt(q_heads[hd], kd_ref[hd // group, kb * kb_len:(kb + 1) * kb_len, :])

    outs = []
    n_items = ATTN_HEADS * ATTN_KEY_BLOCKS
    s_next = scores(0)
    m_run = l_run = o_run = None
    for item in range(n_items):
        hd, kb = divmod(item, ATTN_KEY_BLOCKS)
        s = s_next
        if item + 1 < n_items:
            s_next = scores(item + 1)
        v_blk = vd_ref[hd // group, kb * kb_len:(kb + 1) * kb_len, :]
        m_blk = jnp.max(s, axis=-1, keepdims=True)
        if kb == 0:
            m_run = m_blk
            p = jnp.exp(s - m_run)
            l_run = jnp.sum(p, axis=-1, keepdims=True)
            o_run = _dot(p.astype(BF16), v_blk)
        else:
            m_new = jnp.maximum(m_run, m_blk)
            alpha = jnp.exp(m_run - m_new)
            p = jnp.exp(s - m_new)
            l_run = alpha * l_run + jnp.sum(p, axis=-1, keepdims=True)
            o_run = alpha * o_run + _dot(p.astype(BF16), v_blk)
            m_run = m_new
        if kb == ATTN_KEY_BLOCKS - 1:
            outs.append(o_run / l_run)
    for pair in range(ATTN_HEADS // 2):
        cols = slice(pair * LANES, (pair + 1) * LANES)
        o_pair = jnp.where(low, outs[2 * pair], outs[2 * pair + 1])
        o_ref[:, cols] = (o_pair * _silu(z_ref[:, cols].astype(F32))).astype(o_ref.dtype)


def _attention(h, rope, q_gain, k_gain, blocksum):
    bsz, seq, _ = h.shape
    tq = min(ATTN_TQ, seq)
    prep_rows = min(512, seq)
    cos2, sin2 = rope
    kernel = functools.partial(_attn_kernel, seq=seq, prep_rows=prep_rows)
    return pl.pallas_call(
        kernel,
        grid=(bsz, seq // tq),
        in_specs=[
            pl.BlockSpec((None, tq, GW), lambda b, n: (b, n, G_AQ)),
            pl.BlockSpec((None, tq, GW), lambda b, n: (b, n, G_AZ)),
            pl.BlockSpec((None, seq, 2 * LANES), lambda b, n: (b, 0, G_AKV * 2)),
            pl.BlockSpec((tq, LANES), lambda b, n: (n, 0)),
            pl.BlockSpec((tq, LANES), lambda b, n: (n, 0)),
            pl.BlockSpec((seq, LANES), lambda b, n: (0, 0)),
            pl.BlockSpec((seq, LANES), lambda b, n: (0, 0)),
            pl.BlockSpec((1, GW), lambda b, n: (0, 0)),
            pl.BlockSpec((1, LANES), lambda b, n: (0, 0)),
            pl.BlockSpec((GW, GW), lambda b, n: (0, 0)),
        ],
        out_specs=pl.BlockSpec((None, tq, GW), lambda b, n: (b, n, 0)),
        out_shape=jax.ShapeDtypeStruct((bsz, seq, GW), BF16),
        scratch_shapes=[pltpu.VMEM((2, seq, LANES), BF16), pltpu.VMEM((2, seq, LANES), BF16)],
        compiler_params=_params(("arbitrary", "arbitrary")),
        name="attention",
    )(h, h, h, cos2, sin2, cos2, sin2, q_gain, k_gain, blocksum)


def _scan_constants(chunk, tb, reverse):
    t = np.arange(chunk)[:, None]
    u = np.arange(chunk)[None, :]
    tri = (u >= t) if reverse else (u <= t)
    q_half = 0 if reverse else 1
    masks = []
    levels = []
    h = chunk // 2
    while h >= LIN_SUB:
        same_block = (t // (2 * h)) == (u // (2 * h))
        masks.append(same_block & ((t // h) % 2 == q_half) & ((u // h) % 2 == 1 - q_half))
        levels.append((h, q_half))
        h //= 2
    masks = np.stack(masks, axis=0).astype(np.float32)
    tri_blocks = np.kron(np.eye(tb // chunk), tri)
    return jnp.asarray(tri_blocks, BF16), jnp.asarray(masks, F32), tuple(levels)


def _scan_kernel(*refs, mode, reverse, final, chunk, tb, levels, pre_rows):
    it = iter(refs)
    if mode == "hgrn":
        q_ref, f_ref, v_ref = next(it), next(it), next(it)
        lb_ref = next(it)
    else:
        q_ref, k_ref, v_ref, low_ref = next(it), next(it), next(it), next(it)
        w2_ref, gb_ref = next(it), next(it)
    if final:
        z_ref, of_ref, gain_ref = next(it), next(it), next(it)
    stack_ref, lmask_ref = next(it), next(it)
    o_ref = next(it)
    st_ref, qs_ref, ks_ref, gs_ref, bs_ref = (next(it) for _ in range(5))
    if final:
        os_ref = next(it)

    n = pl.program_id(1)

    @pl.when(n == 0)
    def _reset():
        st_ref[...] = jnp.zeros_like(st_ref)

    if mode == "gla":
        gs_ref[...] = _dot(low_ref[...], w2_ref[...]) + gb_ref[...]
    for r in range(0, tb, pre_rows):
        rows = slice(r, r + pre_rows)
        if mode == "hgrn":
            qr = q_ref[rows, :].astype(F32)
            qs_ref[rows, :] = _silu(qr) * (HGRN_EXPAND ** -0.5)
            f = f_ref[rows, :].astype(F32)
            e = jnp.exp(-jnp.abs(f))
            r = 1.0 / (1.0 + e)
            pos = f >= 0.0
            sig = jnp.where(pos, r, e * r)
            sig_neg = jnp.where(pos, e * r, r)
            lb = lb_ref[...]
            om = 1.0 - lb
            log_sig = jnp.minimum(f, 0.0) + jnp.log(r)
            g_rows = jnp.where(lb > 0.0, jnp.log(lb + om * sig), jnp.log(om) + log_sig)
            ks_ref[rows, :] = om * sig_neg
        else:
            qs_ref[rows, :] = q_ref[rows, :].astype(F32) * (GLA_HEAD_K ** -0.5)
            ks_ref[rows, :] = k_ref[rows, :].astype(F32)
            g_rows = _log_sigmoid(gs_ref[rows, :]) * (1.0 / GLA_GATE_NORMALIZER)
        gs_ref[rows, :] = g_rows
        bs_ref[rows, :] = _sel_dot(stack_ref[...], g_rows)

    nc = tb // chunk
    shift = (chunk - 1) if reverse else 1
    row = lax.broadcasted_iota(jnp.int32, (chunk, LANES), 0)
    rr = lax.broadcasted_iota(jnp.int32, (chunk, chunk), 0)
    cc = lax.broadcasted_iota(jnp.int32, (chunk, chunk), 1)
    diff = (cc - rr) if reverse else (rr - cc)
    sub_edge = (LIN_SUB - 1) if reverse else 0
    keep = (row & (LIN_SUB - 1)) != sub_edge
    end_row = 0 if reverse else chunk - 1

    def body(ci, carry):
        c = (nc - 1 - ci) if reverse else ci
        r0 = pl.multiple_of(c * chunk, chunk)
        heads = range(4)
        lanes = [slice(hh * LANES, (hh + 1) * LANES) for hh in heads]
        q = [qs_ref[pl.ds(r0, chunk), ln] for ln in lanes]
        k = [ks_ref[pl.ds(r0, chunk), ln] for ln in lanes]
        g = [gs_ref[pl.ds(r0, chunk), ln] for ln in lanes]
        v = [v_ref[pl.ds(r0, chunk), ln] for ln in lanes]
        b = [bs_ref[pl.ds(r0, chunk), ln] for ln in lanes]

        o = []
        for hh in heads:
            st = st_ref[hh]
            total = b[hh][end_row:end_row + 1, :]
            o.append(_dot_nt((q[hh] * jnp.exp(b[hh])).astype(BF16), st.astype(BF16)))
            k_end = (k[hh] * jnp.exp(total - b[hh])).astype(BF16)
            st_ref[hh] = jnp.exp(total) * st + _dot_tn(v[hh], k_end)

        scores = [jnp.zeros((chunk, chunk), F32) for _ in heads]
        for li, (h, q_half) in enumerate(levels):
            is_q = ((row // h) & 1) == q_half
            ref_row = h - 1 if reverse else h
            for hh in heads:
                b3 = b[hh].reshape(chunk // (2 * h), 2 * h, LANES)
                d3 = b3 - b3[:, ref_row:ref_row + 1, :]
                log_e = -jnp.abs(d3.reshape(chunk, LANES))
                x = (jnp.where(is_q, q[hh], k[hh]) * jnp.exp(log_e)).astype(BF16)
                scores[hh] = scores[hh] + _dot_nt(x, x) * lmask_ref[li]

        for hh in heads:
            f0 = jnp.where(keep, jnp.exp(g[hh]), 0.0)
            a = k[hh]
            band = jnp.where(diff == 0, jnp.sum(q[hh] * a, axis=-1, keepdims=True), 0.0)
            for d in range(1, LIN_SUB):
                a = f0 * pltpu.roll(a, shift, 0)
                band = jnp.where(diff == d, jnp.sum(q[hh] * a, axis=-1, keepdims=True), band)
            scores[hh] = scores[hh] + band

        for hh in heads:
            out = o[hh] + _dot(scores[hh].astype(BF16), v[hh])
            if final:
                os_ref[pl.ds(r0, chunk), lanes[hh]] = out
            else:
                o_ref[pl.ds(r0, chunk), lanes[hh]] = out
        return carry

    lax.fori_loop(0, nc, body, 0, unroll=2)

    if final:
        o = os_ref[...] + of_ref[...]
        if mode == "hgrn":
            ms = jnp.mean(o * o, axis=-1, keepdims=True)
            y = o * lax.rsqrt(ms + 1e-6) * gain_ref[...]
        else:
            parts = []
            for hh in range(4):
                oh = o[:, hh * LANES:(hh + 1) * LANES]
                ms = jnp.mean(oh * oh, axis=-1, keepdims=True)
                parts.append(oh * lax.rsqrt(ms + 1e-6))
            y = jnp.concatenate(parts, axis=1) * gain_ref[...]
        o_ref[...] = (y * _silu(z_ref[...].astype(F32))).astype(o_ref.dtype)


def _gated_scan(h, mode, reverse, consts, params, o_fwd=None):
    bsz, seq, _ = h.shape
    tb = min(SCAN_TOKENS, seq)
    nb = seq // tb
    chunk = LIN_CHUNK
    stack, lmask, levels = consts
    final = reverse

    def blk(n):
        return (nb - 1 - n) if reverse else n

    def col(group):
        return pl.BlockSpec((None, tb, GW), lambda b, n: (b, blk(n), group))

    def full(arr):
        nd = arr.ndim
        return pl.BlockSpec(arr.shape, lambda b, n: (0,) * nd)

    if mode == "hgrn":
        lower_bound, gain = params
        ins = [h, h, h, lower_bound]
        specs = [col(G_HQ), col(G_HFB if reverse else G_HFF), col(G_HI), full(lower_bound)]
        z_group = G_HZ
    else:
        w2, gb, gain = params
        ins = [h, h, h, h, w2, gb]
        specs = [col(G_GQ), col(G_GK), col(G_GV),
                 pl.BlockSpec((None, tb, LANES), lambda b, n: (b, blk(n), G_MISC * 4 + 1)),
                 full(w2), full(gb)]
        z_group = G_GZ
    if final:
        ins += [h, o_fwd, gain]
        specs += [col(z_group), pl.BlockSpec((None, tb, GW), lambda b, n: (b, blk(n), 0)), full(gain)]
    ins += [stack, lmask]
    specs += [full(stack), full(lmask)]
    scratch = [pltpu.VMEM((4, LANES, LANES), F32)] + [pltpu.VMEM((tb, GW), F32)] * 4
    if final:
        scratch.append(pltpu.VMEM((tb, GW), F32))
    kernel = functools.partial(_scan_kernel, mode=mode, reverse=reverse, final=final,
                               chunk=chunk, tb=tb, levels=levels, pre_rows=stack.shape[0])
    return pl.pallas_call(
        kernel,
        grid=(bsz, nb),
        in_specs=specs,
        out_specs=pl.BlockSpec((None, tb, GW), lambda b, n: (b, blk(n), 0)),
        out_shape=jax.ShapeDtypeStruct((bsz, seq, GW), BF16 if final else F32),
        scratch_shapes=scratch,
        compiler_params=_params(("arbitrary", "arbitrary")),
        name=f"{mode}_{'bwd' if reverse else 'fwd'}",
    )(*ins)


def _ssd_kernel(*refs, reverse, chunk, tb, nb):
    it = iter(refs)
    if not reverse:
        x_ref, xp_ref, xn_ref, bc_ref, bcp_ref, bcn_ref, cw_ref, cb_ref = (next(it) for _ in range(8))
    else:
        ux_ref, ubc_ref = next(it), next(it)
    dt_ref, dtb_ref, acoef_ref, tri_ref, expand_ref = (next(it) for _ in range(5))
    if reverse:
        z_ref, yf_ref, dskip_ref, gain_ref = (next(it) for _ in range(4))
        o_ref = next(it)
    else:
        o_ref, ux_ref, ubc_ref = next(it), next(it), next(it)
    st_ref, dts_ref, as_ref = next(it), next(it), next(it)
    if reverse:
        ys_ref = next(it)
    else:
        ext_ref = next(it)

    n = pl.program_id(1)
    pos = (nb - 1 - n) if reverse else n

    @pl.when(n == 0)
    def _reset():
        st_ref[...] = jnp.zeros_like(st_ref)

    if not reverse:
        halves = ((x_ref, xp_ref, xn_ref, ux_ref), (bc_ref, bcp_ref, bcn_ref, ubc_ref))
        for hi, (cur, prev, nxt, dst) in enumerate(halves):
            cols = slice(hi * GW, (hi + 1) * GW)
            ext_ref[0:HALO, :] = jnp.where(pos > 0, prev[...].astype(F32), 0.0)
            ext_ref[HALO:HALO + tb, :] = cur[...].astype(F32)
            ext_ref[HALO + tb:2 * HALO + tb, :] = jnp.where(pos < nb - 1, nxt[...].astype(F32), 0.0)
            acc = jnp.broadcast_to(cb_ref[:, cols], (tb, GW))
            for tap in range(SSD_CONV_WIDTH):
                acc = acc + cw_ref[tap:tap + 1, cols] * ext_ref[HALO - 2 + tap:HALO - 2 + tap + tb, :]
            dst[...] = _silu(acc).astype(dst.dtype)

    dt = _softplus(dt_ref[...].astype(F32) + dtb_ref[...])
    dts_ref[...] = dt
    as_ref[...] = dt * acoef_ref[...]

    nc = tb // chunk
    head0 = SSD_HEADS if reverse else 0
    rr = lax.broadcasted_iota(jnp.int32, (chunk, chunk), 0)
    cc = lax.broadcasted_iota(jnp.int32, (chunk, chunk), 1)
    valid = (rr <= cc) if reverse else (rr >= cc)
    lane = lax.broadcasted_iota(jnp.int32, (chunk, LANES), 1)
    low = lane < SSD_HEAD_DIM
    end_row = 0 if reverse else chunk - 1

    def body(ci, carry):
        c = (nc - 1 - ci) if reverse else ci
        r0 = pl.multiple_of(c * chunk, chunk)
        rows = pl.ds(r0, chunk)
        dtc = dts_ref[rows, :]
        acs = _sel_dot(tri_ref[...], as_ref[rows, :])
        a_end = acs[end_row:end_row + 1, :]
        acs_t = jnp.transpose(acs)
        narrow = jnp.concatenate([dtc, jnp.exp(acs), jnp.exp(a_end - acs)], axis=0)
        hi = narrow.astype(BF16)
        lo = (narrow - hi.astype(F32)).astype(BF16)
        wide = _dot(hi, expand_ref[...]) + _dot(lo, expand_ref[...])
        dt_x = wide[0:chunk]
        e_acs = wide[chunk:2 * chunk]
        e_rest = wide[2 * chunk:3 * chunk]

        for g in range(SSD_GROUPS):
            bg = ubc_ref[rows, g * LANES:(g + 1) * LANES]
            cg = ubc_ref[rows, (SSD_GROUPS + g) * LANES:(SSD_GROUPS + g + 1) * LANES]
            gm = _dot_nt(cg, bg)
            st = st_ref[g]
            c_st = _dot(cg, st.astype(BF16))
            wx, e_end = [], []
            for jp in range(2):
                pair = 2 * g + jp
                pcols = slice(pair * LANES, (pair + 1) * LANES)
                ea = e_acs[:, pcols]
                xdt = ux_ref[rows, pcols] * dt_x[:, pcols]
                wx.append((e_rest[:, pcols] * xdt).astype(BF16))
                e_end.append(ea[end_row:end_row + 1, :])
                acc = c_st[:, jp * LANES:(jp + 1) * LANES] * ea
                for e in range(2):
                    hl = head0 + 2 * pair + e
                    dmat = acs[:, hl:hl + 1] - acs_t[hl:hl + 1, :]
                    lm = jnp.where(valid, jnp.exp(jnp.minimum(dmat, 0.0)), 0.0)
                    m = (gm * lm).astype(BF16)
                    xm = jnp.where(low if e == 0 else jnp.logical_not(low), xdt, 0.0).astype(BF16)
                    acc = acc + _dot(m, xm)
                if reverse:
                    ys_ref[rows, pcols] = acc
                else:
                    o_ref[rows, pcols] = acc
            st_ref[g] = (jnp.concatenate(e_end, axis=1) * st
                         + _dot_tn(bg, jnp.concatenate(wx, axis=1)))
        return carry

    lax.fori_loop(0, nc, body, 0)

    if reverse:
        y = ys_ref[...] + yf_ref[...] + dskip_ref[...] * ux_ref[...]
        y = y * _silu(z_ref[...].astype(F32))
        ms = jnp.mean(y * y, axis=-1, keepdims=True)
        o_ref[...] = (y * lax.rsqrt(ms + 1e-6) * gain_ref[...]).astype(o_ref.dtype)


def _ssd(h, reverse, params, fwd=None):
    bsz, seq, _ = h.shape
    tb = min(SCAN_TOKENS, seq)
    nb = seq // tb
    chunk = min(SSD_CHUNK, tb)
    conv_w, conv_b, dt_bias, a_coef, tri, expand, d_skip, gain = params
    hb = tb // HALO
    nhb = seq // HALO

    def blk(n):
        return (nb - 1 - n) if reverse else n

    def col(group, width=GW, scale=1):
        return pl.BlockSpec((None, tb, width), lambda b, n: (b, blk(n), group * scale))

    def prev(group):
        return pl.BlockSpec((None, HALO, GW), lambda b, n: (b, jnp.maximum(blk(n) * hb - 1, 0), group))

    def nxt(group):
        return pl.BlockSpec((None, HALO, GW), lambda b, n: (b, jnp.minimum((blk(n) + 1) * hb, nhb - 1), group))

    def full(arr):
        nd = arr.ndim
        return pl.BlockSpec(arr.shape, lambda b, n: (0,) * nd)

    plain = pl.BlockSpec((None, tb, GW), lambda b, n: (b, blk(n), 0))
    dt_spec = pl.BlockSpec((None, tb, LANES), lambda b, n: (b, blk(n), G_MISC * 4))
    state = pltpu.VMEM((SSD_GROUPS, SSD_STATE, 2 * LANES), F32)
    narrow = pltpu.VMEM((tb, LANES), F32)
    kernel = functools.partial(_ssd_kernel, reverse=reverse, chunk=chunk, tb=tb, nb=nb)
    if not reverse:
        return pl.pallas_call(
            kernel,
            grid=(bsz, nb),
            in_specs=[col(G_SX), prev(G_SX), nxt(G_SX), col(G_SBC), prev(G_SBC), nxt(G_SBC),
                      full(conv_w), full(conv_b), dt_spec, full(dt_bias), full(a_coef), full(tri),
                      full(expand)],
            out_specs=[plain, plain, plain],
            out_shape=[jax.ShapeDtypeStruct((bsz, seq, GW), F32),
                       jax.ShapeDtypeStruct((bsz, seq, GW), F32),
                       jax.ShapeDtypeStruct((bsz, seq, GW), BF16)],
            scratch_shapes=[state, narrow, narrow, pltpu.VMEM((tb + 2 * HALO, GW), F32)],
            compiler_params=_params(("arbitrary", "arbitrary")),
            name="ssd_fwd",
        )(h, h, h, h, h, h, conv_w, conv_b, h, dt_bias, a_coef, tri, expand)
    y_fwd, u_x, u_bc = fwd
    return pl.pallas_call(
        kernel,
        grid=(bsz, nb),
        in_specs=[plain, plain, dt_spec, full(dt_bias), full(a_coef), full(tri), full(expand),
                  col(G_SZ), plain, full(d_skip), full(gain)],
        out_specs=plain,
        out_shape=jax.ShapeDtypeStruct((bsz, seq, GW), BF16),
        scratch_shapes=[state, narrow, narrow, pltpu.VMEM((tb, GW), F32)],
        compiler_params=_params(("arbitrary", "arbitrary")),
        name="ssd_bwd",
    )(u_x, u_bc, h, dt_bias, a_coef, tri, expand, h, y_fwd, d_skip, gain)


def _out_kernel(ya_ref, yh_ref, ys_ref, yg_ref, x_ref, w_ref, g_ref, b_ref, xo_ref, xb_ref):
    acc = _dot(ya_ref[...], w_ref[0:GW, :])
    acc = acc + _dot(yh_ref[...], w_ref[GW:2 * GW, :])
    acc = acc + _dot(ys_ref[...], w_ref[2 * GW:3 * GW, :])
    acc = acc + _dot(yg_ref[...], w_ref[3 * GW:4 * GW, :])
    r = DEEPNORM_ALPHA * x_ref[...] + acc
    mu = jnp.mean(r, axis=-1, keepdims=True)
    rc = r - mu
    var = jnp.mean(rc * rc, axis=-1, keepdims=True)
    y = rc * lax.rsqrt(var + 1e-5) * g_ref[...] + b_ref[...]
    xo_ref[...] = y
    xb_ref[...] = y.astype(BF16)


def _out_proj_ln(ya, yh, ys, yg, x, w, ln_g, ln_b):
    t, d = x.shape
    tm = min(512, t)
    row = lambda width: pl.BlockSpec((tm, width), lambda i: (i, 0))
    const = lambda arr: pl.BlockSpec(arr.shape, lambda i: (0, 0))
    return pl.pallas_call(
        _out_kernel,
        grid=(t // tm,),
        in_specs=[row(GW), row(GW), row(GW), row(GW), row(d), const(w), const(ln_g), const(ln_b)],
        out_specs=[row(d), row(d)],
        out_shape=[jax.ShapeDtypeStruct((t, d), F32), jax.ShapeDtypeStruct((t, d), BF16)],
        compiler_params=_params(("parallel",)),
        name="out_proj_ln",
    )(ya, yh, ys, yg, x, w, ln_g, ln_b)


def _pad_heads(w, heads, width):
    d = w.shape[0]
    w = w.reshape(d, heads, width)
    w = jnp.pad(w, ((0, 0), (0, 0), (0, LANES - width)))
    return w.reshape(d, heads * LANES)


def _layout_w_in(w):
    d = w.shape[0]
    sizes = (512, 128, 128, 512, 512, 512, 512, 512, 512, 1024, 8, 8, 512, 256, 256, 512, 16, 16, 512)
    offs = np.concatenate([[0], np.cumsum(sizes)])
    (a_q, a_k, a_v, a_z, h_q, h_ff, h_fb, h_i, h_z,
     s_xbc, s_dtf, s_dtb, s_z, g_q, g_k, g_v, g_lf, g_lb, g_z) = [
        w[:, int(offs[i]):int(offs[i + 1])] for i in range(len(sizes))]
    zeros = lambda n: jnp.zeros((d, n), w.dtype)
    groups = [
        a_q, a_z, jnp.concatenate([a_k, a_v, zeros(256)], axis=1),
        h_q, h_ff, h_fb, h_i, h_z,
        s_xbc[:, :512], s_xbc[:, 512:], s_z,
        _pad_heads(g_q, GLA_HEADS, GLA_HEAD_K), _pad_heads(g_k, GLA_HEADS, GLA_HEAD_K), g_v, g_z,
        jnp.concatenate([s_dtf, s_dtb, zeros(112), g_lf, g_lb, zeros(96), zeros(256)], axis=1),
    ]
    return jnp.concatenate(groups, axis=1).astype(BF16)


def _rope_tables(seq):
    rows = seq // GRID_W
    row_pos = jnp.repeat(jnp.arange(rows, dtype=F32), GRID_W)
    col_pos = jnp.tile(jnp.arange(GRID_W, dtype=F32), rows)
    axis_dim = ATTN_HEAD_DIM // 2
    inv_freq = jnp.power(ROPE_THETA, -jnp.arange(0, axis_dim, 2, dtype=F32) / axis_dim)
    ang_r = row_pos[:, None] * inv_freq
    ang_c = col_pos[:, None] * inv_freq
    cos = jnp.concatenate([jnp.cos(ang_r)] * 2 + [jnp.cos(ang_c)] * 2, axis=1)
    sin = jnp.concatenate([-jnp.sin(ang_r), jnp.sin(ang_r), -jnp.sin(ang_c), jnp.sin(ang_c)], axis=1)
    return jnp.tile(cos, (1, 2)), jnp.tile(sin, (1, 2))


def _ssd_expand(reverse):
    m = np.zeros((LANES, GW), np.float32)
    for hd in range(SSD_HEADS):
        m[hd + (SSD_HEADS if reverse else 0), hd * SSD_HEAD_DIM:(hd + 1) * SSD_HEAD_DIM] = 1.0
    return jnp.asarray(m, BF16)


def _tri(chunk, reverse):
    t = np.arange(chunk)[:, None]
    u = np.arange(chunk)[None, :]
    return jnp.asarray((u >= t) if reverse else (u <= t), BF16)


def kernel(x, w_in, attn_q_norm, attn_k_norm, hgrn_lb_logits, hgrn_norm, ssd_conv_w, ssd_conv_b,
           ssd_dt_bias, ssd_a_log, ssd_d, ssd_norm, gla_gk_w2, gla_gk_b, gla_norm, w_out, ln_g, ln_b):
    bsz, seq, d = x.shape
    t = bsz * seq
    rope = _rope_tables(seq)
    blocksum = jnp.asarray(np.kron(np.eye(GW // ATTN_HEAD_DIM), np.ones((ATTN_HEAD_DIM, ATTN_HEAD_DIM))), BF16)
    lower_bounds = jnp.cumsum(jax.nn.softmax(hgrn_lb_logits.astype(F32), axis=0), axis=0)
    lower_bounds = lower_bounds - lower_bounds[0]
    scan_consts = {rev: _scan_constants(LIN_CHUNK, min(SCAN_PRE_ROWS, seq), rev) for rev in (False, True)}
    ssd_chunk = min(SSD_CHUNK, min(SCAN_TOKENS, seq))

    xf = x.reshape(t, d)
    xb = xf.astype(BF16)
    for i in range(DEPTH):
        h = _in_proj(xb, _layout_w_in(w_in[i])).reshape(bsz, seq, N_PROJ)

        y_a = _attention(h, rope, jnp.tile(attn_q_norm[i], ATTN_HEADS)[None, :].astype(F32),
                         jnp.tile(attn_k_norm[i], 2)[None, :].astype(F32), blocksum)

        lb = jnp.maximum(lower_bounds[i], 0.0)[None, :]
        hgrn_params = (lb, hgrn_norm[i][None, :].astype(F32))
        o_f = _gated_scan(h, "hgrn", False, scan_consts[False], hgrn_params)
        y_h = _gated_scan(h, "hgrn", True, scan_consts[True], hgrn_params, o_f)

        gla_out = None
        for rev in (False, True):
            w2 = jnp.zeros((LANES, GW), F32)
            w2 = w2.at[rev * GLA_GATE_RANK:(rev + 1) * GLA_GATE_RANK].set(
                _pad_heads(gla_gk_w2[i, int(rev)], GLA_HEADS, GLA_HEAD_K))
            gb = _pad_heads(gla_gk_b[i, int(rev)][None, :], GLA_HEADS, GLA_HEAD_K)
            gla_params = (w2.astype(BF16), gb.astype(F32), jnp.tile(gla_norm[i], GLA_HEADS)[None, :].astype(F32))
            gla_out = _gated_scan(h, "gla", rev, scan_consts[rev], gla_params, gla_out)
        y_g = gla_out

        dt_bias = jnp.zeros((1, LANES), F32).at[0, :2 * SSD_HEADS].set(ssd_dt_bias[i].astype(F32).reshape(-1))
        a_coef = jnp.zeros((1, LANES), F32).at[0, :2 * SSD_HEADS].set(-jnp.exp(ssd_a_log[i].astype(F32)).reshape(-1))
        ssd_out = None
        for rev in (False, True):
            ssd_params = (ssd_conv_w[i].astype(F32), ssd_conv_b[i][None, :].astype(F32), dt_bias, a_coef,
                          _tri(ssd_chunk, rev), _ssd_expand(rev), jnp.repeat(ssd_d[i].astype(F32), SSD_HEAD_DIM)[None, :],
                          ssd_norm[i][None, :].astype(F32))
            ssd_out = _ssd(h, rev, ssd_params, ssd_out)
        y_s = ssd_out

        flat = lambda a: a.reshape(t, GW)
        xf, xb = _out_proj_ln(flat(y_a), flat(y_h), flat(y_s), flat(y_g), xf, w_out[i].astype(BF16),
                              ln_g[i][None, :].astype(F32), ln_b[i][None, :].astype(F32))
    return xf.reshape(bsz, seq, d)
```

```python
import functools

import numpy as np
import jax
import jax.numpy as jnp
from jax import lax
from jax.experimental import pallas as pl
from jax.experimental.pallas import tpu as pltpu

F32 = jnp.float32
BF16 = jnp.bfloat16

D_MODEL = 1024
DEPTH = 2
GROUP_WIDTH = 512
GRID_W = 64
ATTN_HEADS = 8
ATTN_KV_HEADS = 2
ATTN_HEAD_DIM = 64
ROPE_THETA = 10000.0
HGRN_HEADS = 4
HGRN_EXPAND = 128
SSD_HEADS = 8
SSD_HEAD_DIM = 64
SSD_GROUPS = 2
SSD_STATE = 128
SSD_CONV_WIDTH = 5
GLA_HEADS = 4
GLA_HEAD_K = 64
GLA_HEAD_V = 128
GLA_GATE_RANK = 16
GLA_GATE_NORMALIZER = 16.0
DEEPNORM_ALPHA = (2 * DEPTH) ** 0.25

LANES = 128
SUBLANES = 8
HALO = 16
GW = GROUP_WIDTH

(G_AQ, G_AZ, G_AKV, G_HQ, G_HFF, G_HFB, G_HI, G_HZ,
 G_SX, G_SBC, G_SZ, G_GQ, G_GK, G_GV, G_GZ) = range(15)
N_GROUPS = 15
NARROW_DT = G_AKV * 4 + 2
NARROW_LOW = G_AKV * 4 + 3
N_PROJ = N_GROUPS * GW

LIN_CHUNK = 64
LIN_SUB = 4
SSD_CHUNK = 128
SCAN_TOKENS = 512
SCAN_PRE_ROWS = 128
ATTN_TQ = 256
ATTN_KEY_BLOCKS = 2
VMEM_LIMIT = 48 * 1024 * 1024


def _dot(a, b):
    return jnp.dot(a, b, preferred_element_type=F32)


def _dot_nt(a, b):
    return lax.dot_general(a, b, (((1,), (1,)), ((), ())), preferred_element_type=F32)


def _dot_tn(a, b):
    return lax.dot_general(a, b, (((0,), (0,)), ((), ())), preferred_element_type=F32)


def _split3(x):
    hi = x.astype(BF16)
    r1 = x - hi.astype(F32)
    mid = r1.astype(BF16)
    lo = (r1 - mid.astype(F32)).astype(BF16)
    return hi, mid, lo


def _sel_dot(sel, x):
    hi, mid, lo = _split3(x)
    return _dot(sel, hi) + _dot(sel, mid) + _dot(sel, lo)


def _log_sigmoid(x):
    return jnp.minimum(x, 0.0) - jnp.log1p(jnp.exp(-jnp.abs(x)))


def _silu(x):
    return x * jax.nn.sigmoid(x)


def _softplus(x):
    return jnp.maximum(x, 0.0) + jnp.log1p(jnp.exp(-jnp.abs(x)))


def _params(sem):
    return pltpu.CompilerParams(dimension_semantics=sem, vmem_limit_bytes=VMEM_LIMIT)


def _matmul_kernel(x_ref, w_ref, o_ref):
    o_ref[...] = _dot(x_ref[...].astype(BF16), w_ref[...]).astype(o_ref.dtype)


def _in_proj(xb, w):
    t, d = xb.shape
    n = w.shape[1]
    tm = min(2048, t)
    tn = GW
    return pl.pallas_call(
        _matmul_kernel,
        grid=(t // tm, n // tn),
        in_specs=[pl.BlockSpec((tm, d), lambda i, j: (i, 0)),
                  pl.BlockSpec((d, tn), lambda i, j: (0, j))],
        out_specs=pl.BlockSpec((tm, tn), lambda i, j: (i, j)),
        out_shape=jax.ShapeDtypeStruct((t, n), BF16),
        compiler_params=_params(("parallel", "parallel")),
        name="in_proj",
    )(xb, w)


def _head_rms(x, blocksum, gain, eps=1e-6):
    xx = x * x
    hi = xx.astype(BF16)
    lo = (xx - hi.astype(F32)).astype(BF16)
    ssq = _dot(hi, blocksum) + _dot(lo, blocksum)
    return x * lax.rsqrt(ssq * (1.0 / ATTN_HEAD_DIM) + eps) * gain


def _rope(x, cos, sin_signed):
    w = x.shape[1]
    lane = lax.broadcasted_iota(jnp.int32, x.shape, 1)
    first = (lane & 31) < 16
    partner = jnp.where(first, pltpu.roll(x, w - 16, 1), pltpu.roll(x, 16, 1))
    return x * cos + partner * sin_signed


def _attn_kernel(q_ref, z_ref, kv_ref, cos_ref, sin_ref, cosk_ref, sink_ref, qg_ref, kg_ref,
                 bs_ref, o_ref, kd_ref, vd_ref, *, seq, prep_rows):
    n = pl.program_id(1)

    @pl.when(n == 0)
    def _prep_kv():
        def body(i, carry):
            r0 = pl.multiple_of(i * prep_rows, prep_rows)
            kv = kv_ref[pl.ds(r0, prep_rows), :].astype(F32)
            k = kv[:, :LANES]
            v = kv[:, LANES:]
            k = _head_rms(k, bs_ref[0:LANES, 0:LANES], kg_ref[...])
            k = _rope(k, cosk_ref[pl.ds(r0, prep_rows), :], sink_ref[pl.ds(r0, prep_rows), :])
            lane = lax.broadcasted_iota(jnp.int32, k.shape, 1)
            low = lane < ATTN_HEAD_DIM
            k_sw = pltpu.roll(k, ATTN_HEAD_DIM, 1)
            v_sw = pltpu.roll(v, ATTN_HEAD_DIM, 1)
            kd_ref[0, pl.ds(r0, prep_rows), :] = jnp.where(low, k, k_sw).astype(BF16)
            kd_ref[1, pl.ds(r0, prep_rows), :] = jnp.where(low, k_sw, k).astype(BF16)
            vd_ref[0, pl.ds(r0, prep_rows), :] = jnp.where(low, v, v_sw).astype(BF16)
            vd_ref[1, pl.ds(r0, prep_rows), :] = jnp.where(low, v_sw, v).astype(BF16)
            return carry
        lax.fori_loop(0, seq // prep_rows, body, 0)

    q = _head_rms(q_ref[...].astype(F32), bs_ref[...], qg_ref[...])
    cos = jnp.concatenate([cos_ref[...]] * 4, axis=1)
    sin = jnp.concatenate([sin_ref[...]] * 4, axis=1)
    q = _rope(q, cos, sin) * (ATTN_HEAD_DIM ** -0.5)
    lane = lax.broadcasted_iota(jnp.int32, (q.shape[0], LANES), 1)
    low = lane < ATTN_HEAD_DIM
    group = ATTN_HEADS // ATTN_KV_HEADS
    kb_len = seq // ATTN_KEY_BLOCKS

    q_heads = []
    for hd in range(ATTN_HEADS):
        pair, e = divmod(hd, 2)
        qp = q[:, pair * LANES:(pair + 1) * LANES]
        q_heads.append(jnp.where(low if e == 0 else jnp.logical_not(low), qp, 0.0).astype(BF16))

    def scores(item):
        hd, kb = divmod(item, ATTN_KEY_BLOCKS)
        return _dot_nt(q_heads[hd], kd_ref[hd // group, kb * kb_len:(kb + 1) * kb_len, :])

    outs = []
    n_items = ATTN_HEADS * ATTN_KEY_BLOCKS
    s_next = scores(0)
    m_run = l_run = o_run = None
    for item in range(n_items):
        hd, kb = divmod(item, ATTN_KEY_BLOCKS)
        s = s_next
        if item + 1 < n_items:
            s_next = scores(item + 1)
        v_blk = vd_ref[hd // group, kb * kb_len:(kb + 1) * kb_len, :]
        m_blk = jnp.max(s, axis=-1, keepdims=True)
        if kb == 0:
            m_run = m_blk
            p = jnp.exp(s - m_run)
            l_run = jnp.sum(p, axis=-1, keepdims=True)
            o_run = _dot(p.astype(BF16), v_blk)
        else:
            m_new = jnp.maximum(m_run, m_blk)
            alpha = jnp.exp(m_run - m_new)
            p = jnp.exp(s - m_new)
            l_run = alpha * l_run + jnp.sum(p, axis=-1, keepdims=True)
            o_run = alpha * o_run + _dot(p.astype(BF16), v_blk)
            m_run = m_new
        if kb == ATTN_KEY_BLOCKS - 1:
            outs.append(o_run / l_run)
    for pair in range(ATTN_HEADS // 2):
        cols = slice(pair * LANES, (pair + 1) * LANES)
        o_pair = jnp.where(low, outs[2 * pair], outs[2 * pair + 1])
        o_ref[:, cols] = (o_pair * _silu(z_ref[:, cols].astype(F32))).astype(o_ref.dtype)


def _attention(h, rope, q_gain, k_gain, blocksum):
    bsz, seq, _ = h.shape
    tq = min(ATTN_TQ, seq)
    prep_rows = min(512, seq)
    cos2, sin2 = rope
    kernel = functools.partial(_attn_kernel, seq=seq, prep_rows=prep_rows)
    return pl.pallas_call(
        kernel,
        grid=(bsz, seq // tq),
        in_specs=[
            pl.BlockSpec((None, tq, GW), lambda b, n: (b, n, G_AQ)),
            pl.BlockSpec((None, tq, GW), lambda b, n: (b, n, G_AZ)),
            pl.BlockSpec((None, seq, 2 * LANES), lambda b, n: (b, 0, G_AKV * 2)),
            pl.BlockSpec((tq, LANES), lambda b, n: (n, 0)),
            pl.BlockSpec((tq, LANES), lambda b, n: (n, 0)),
            pl.BlockSpec((seq, LANES), lambda b, n: (0, 0)),
            pl.BlockSpec((seq, LANES), lambda b, n: (0, 0)),
            pl.BlockSpec((1, GW), lambda b, n: (0, 0)),
            pl.BlockSpec((1, LANES), lambda b, n: (0, 0)),
            pl.BlockSpec((GW, GW), lambda b, n: (0, 0)),
        ],
        out_specs=pl.BlockSpec((None, tq, GW), lambda b, n: (b, n, 0)),
        out_shape=jax.ShapeDtypeStruct((bsz, seq, GW), BF16),
        scratch_shapes=[pltpu.VMEM((2, seq, LANES), BF16), pltpu.VMEM((2, seq, LANES), BF16)],
        compiler_params=_params(("arbitrary", "arbitrary")),
        name="attention",
    )(h, h, h, cos2, sin2, cos2, sin2, q_gain, k_gain, blocksum)


def _scan_constants(chunk, tb, reverse):
    t = np.arange(chunk)[:, None]
    u = np.arange(chunk)[None, :]
    tri = (u >= t) if reverse else (u <= t)
    q_half = 0 if reverse else 1
    masks = []
    levels = []
    h = chunk // 2
    while h >= LIN_SUB:
        same_block = (t // (2 * h)) == (u // (2 * h))
        masks.append(same_block & ((t // h) % 2 == q_half) & ((u // h) % 2 == 1 - q_half))
        levels.append((h, q_half))
        h //= 2
    masks = np.stack(masks, axis=0).astype(np.float32)
    tri_blocks = np.kron(np.eye(tb // chunk), tri)
    return jnp.asarray(tri_blocks, BF16), jnp.asarray(masks, F32), tuple(levels)


def _scan_kernel(*refs, mode, reverse, final, chunk, tb, levels, pre_rows):
    it = iter(refs)
    if mode == "hgrn":
        q_ref, f_ref, v_ref = next(it), next(it), next(it)
        lb_ref = next(it)
    else:
        q_ref, k_ref, v_ref, low_ref = next(it), next(it), next(it), next(it)
        w2_ref, gb_ref = next(it), next(it)
    if final:
        z_ref, of_ref, gain_ref = next(it), next(it), next(it)
    stack_ref, lmask_ref = next(it), next(it)
    o_ref = next(it)
    st_ref, qs_ref, ks_ref, gs_ref, bs_ref = (next(it) for _ in range(5))
    if final:
        os_ref = next(it)

    n = pl.program_id(1)

    @pl.when(n == 0)
    def _reset():
        st_ref[...] = jnp.zeros_like(st_ref)

    if mode == "gla":
        gs_ref[...] = _dot(low_ref[...], w2_ref[...]) + gb_ref[...]
    for r in range(0, tb, pre_rows):
        rows = slice(r, r + pre_rows)
        if mode == "hgrn":
            qr = q_ref[rows, :].astype(F32)
            qs_ref[rows, :] = _silu(qr) * (HGRN_EXPAND ** -0.5)
            f = f_ref[rows, :].astype(F32)
            e = jnp.exp(-jnp.abs(f))
            r = 1.0 / (1.0 + e)
            pos = f >= 0.0
            sig = jnp.where(pos, r, e * r)
            sig_neg = jnp.where(pos, e * r, r)
            lb = lb_ref[...]
            om = 1.0 - lb
            log_sig = jnp.minimum(f, 0.0) + jnp.log(r)
            g_rows = jnp.where(lb > 0.0, jnp.log(lb + om * sig), jnp.log(om) + log_sig)
            ks_ref[rows, :] = om * sig_neg
        else:
            qs_ref[rows, :] = q_ref[rows, :].astype(F32) * (GLA_HEAD_K ** -0.5)
            ks_ref[rows, :] = k_ref[rows, :].astype(F32)
            g_rows = _log_sigmoid(gs_ref[rows, :]) * (1.0 / GLA_GATE_NORMALIZER)
        gs_ref[rows, :] = g_rows
        bs_ref[rows, :] = _sel_dot(stack_ref[...], g_rows)

    nc = tb // chunk
    shift = (chunk - 1) if reverse else 1
    row = lax.broadcasted_iota(jnp.int32, (chunk, LANES), 0)
    rr = lax.broadcasted_iota(jnp.int32, (chunk, chunk), 0)
    cc = lax.broadcasted_iota(jnp.int32, (chunk, chunk), 1)
    diff = (cc - rr) if reverse else (rr - cc)
    sub_edge = (LIN_SUB - 1) if reverse else 0
    keep = (row & (LIN_SUB - 1)) != sub_edge
    end_row = 0 if reverse else chunk - 1

    def body(ci, carry):
        c = (nc - 1 - ci) if reverse else ci
        r0 = pl.multiple_of(c * chunk, chunk)
        heads = range(4)
        lanes = [slice(hh * LANES, (hh + 1) * LANES) for hh in heads]
        q = [qs_ref[pl.ds(r0, chunk), ln] for ln in lanes]
        k = [ks_ref[pl.ds(r0, chunk), ln] for ln in lanes]
        g = [gs_ref[pl.ds(r0, chunk), ln] for ln in lanes]
        v = [v_ref[pl.ds(r0, chunk), ln] for ln in lanes]
        b = [bs_ref[pl.ds(r0, chunk), ln] for ln in lanes]

        o = []
        for hh in heads:
            st = st_ref[hh]
            total = b[hh][end_row:end_row + 1, :]
            o.append(_dot_nt((q[hh] * jnp.exp(b[hh])).astype(BF16), st.astype(BF16)))
            k_end = (k[hh] * jnp.exp(total - b[hh])).astype(BF16)
            st_ref[hh] = jnp.exp(total) * st + _dot_tn(v[hh], k_end)

        scores = [jnp.zeros((chunk, chunk), F32) for _ in heads]
        for li, (h, q_half) in enumerate(levels):
            is_q = ((row // h) & 1) == q_half
            ref_row = h - 1 if reverse else h
            for hh in heads:
                b3 = b[hh].reshape(chunk // (2 * h), 2 * h, LANES)
                d3 = b3 - b3[:, ref_row:ref_row + 1, :]
                log_e = -jnp.abs(d3.reshape(chunk, LANES))
                x = (jnp.where(is_q, q[hh], k[hh]) * jnp.exp(log_e)).astype(BF16)
                scores[hh] = scores[hh] + _dot_nt(x, x) * lmask_ref[li]

        for hh in heads:
            f0 = jnp.where(keep, jnp.exp(g[hh]), 0.0)
            a = k[hh]
            band = jnp.where(diff == 0, jnp.sum(q[hh] * a, axis=-1, keepdims=True), 0.0)
            for d in range(1, LIN_SUB):
                a = f0 * pltpu.roll(a, shift, 0)
                band = jnp.where(diff == d, jnp.sum(q[hh] * a, axis=-1, keepdims=True), band)
            scores[hh] = scores[hh] + band

        for hh in heads:
            out = o[hh] + _dot(scores[hh].astype(BF16), v[hh])
            if final:
                os_ref[pl.ds(r0, chunk), lanes[hh]] = out
            else:
                o_ref[pl.ds(r0, chunk), lanes[hh]] = out
        return carry

    lax.fori_loop(0, nc, body, 0, unroll=2)

    if final:
        o = os_ref[...] + of_ref[...]
        if mode == "hgrn":
            ms = jnp.mean(o * o, axis=-1, keepdims=True)
            y = o * lax.rsqrt(ms + 1e-6) * gain_ref[...]
        else:
            parts = []
            for hh in range(4):
                oh = o[:, hh * LANES:(hh + 1) * LANES]
                ms = jnp.mean(oh * oh, axis=-1, keepdims=True)
                parts.append(oh * lax.rsqrt(ms + 1e-6))
            y = jnp.concatenate(parts, axis=1) * gain_ref[...]
        o_ref[...] = (y * _silu(z_ref[...].astype(F32))).astype(o_ref.dtype)


def _gated_scan(h, mode, reverse, consts, params, o_fwd=None):
    bsz, seq, _ = h.shape
    tb = min(SCAN_TOKENS, seq)
    nb = seq // tb
    chunk = LIN_CHUNK
    stack, lmask, levels = consts
    final = reverse

    def blk(n):
        return (nb - 1 - n) if reverse else n

    def col(group):
        return pl.BlockSpec((None, tb, GW), lambda b, n: (b, blk(n), group))

    def full(arr):
        nd = arr.ndim
        return pl.BlockSpec(arr.shape, lambda b, n: (0,) * nd)

    if mode == "hgrn":
        lower_bound, gain = params
        ins = [h, h, h, lower_bound]
        specs = [col(G_HQ), col(G_HFB if reverse else G_HFF), col(G_HI), full(lower_bound)]
        z_group = G_HZ
    else:
        w2, gb, gain = params
        ins = [h, h, h, h, w2, gb]
        specs = [col(G_GQ), col(G_GK), col(G_GV),
                 pl.BlockSpec((None, tb, LANES), lambda b, n: (b, blk(n), NARROW_LOW)),
                 full(w2), full(gb)]
        z_group = G_GZ
    if final:
        ins += [h, o_fwd, gain]
        specs += [col(z_group), pl.BlockSpec((None, tb, GW), lambda b, n: (b, blk(n), 0)), full(gain)]
    ins += [stack, lmask]
    specs += [full(stack), full(lmask)]
    scratch = [pltpu.VMEM((4, LANES, LANES), F32)] + [pltpu.VMEM((tb, GW), F32)] * 4
    if final:
        scratch.append(pltpu.VMEM((tb, GW), F32))
    kernel = functools.partial(_scan_kernel, mode=mode, reverse=reverse, final=final,
                               chunk=chunk, tb=tb, levels=levels, pre_rows=stack.shape[0])
    return pl.pallas_call(
        kernel,
        grid=(bsz, nb),
        in_specs=specs,
        out_specs=pl.BlockSpec((None, tb, GW), lambda b, n: (b, blk(n), 0)),
        out_shape=jax.ShapeDtypeStruct((bsz, seq, GW), BF16 if final else F32),
        scratch_shapes=scratch,
        compiler_params=_params(("arbitrary", "arbitrary")),
        name=f"{mode}_{'bwd' if reverse else 'fwd'}",
    )(*ins)


def _ssd_kernel(*refs, reverse, chunk, tb, nb):
    it = iter(refs)
    if not reverse:
        x_ref, xp_ref, xn_ref, bc_ref, bcp_ref, bcn_ref, cw_ref, cb_ref = (next(it) for _ in range(8))
    else:
        ux_ref, ubc_ref = next(it), next(it)
    dt_ref, dtb_ref, acoef_ref, tri_ref, expand_ref = (next(it) for _ in range(5))
    if reverse:
        z_ref, yf_ref, dskip_ref, gain_ref = (next(it) for _ in range(4))
        o_ref = next(it)
    else:
        o_ref, ux_ref, ubc_ref = next(it), next(it), next(it)
    st_ref, dts_ref, as_ref = next(it), next(it), next(it)
    if reverse:
        ys_ref = next(it)
    else:
        ext_ref = next(it)

    n = pl.program_id(1)
    pos = (nb - 1 - n) if reverse else n

    @pl.when(n == 0)
    def _reset():
        st_ref[...] = jnp.zeros_like(st_ref)

    if not reverse:
        halves = ((x_ref, xp_ref, xn_ref, ux_ref), (bc_ref, bcp_ref, bcn_ref, ubc_ref))
        for hi, (cur, prev, nxt, dst) in enumerate(halves):
            cols = slice(hi * GW, (hi + 1) * GW)
            ext_ref[0:HALO, :] = jnp.where(pos > 0, prev[...].astype(F32), 0.0)
            ext_ref[HALO:HALO + tb, :] = cur[...].astype(F32)
            ext_ref[HALO + tb:2 * HALO + tb, :] = jnp.where(pos < nb - 1, nxt[...].astype(F32), 0.0)
            acc = jnp.broadcast_to(cb_ref[:, cols], (tb, GW))
            for tap in range(SSD_CONV_WIDTH):
                acc = acc + cw_ref[tap:tap + 1, cols] * ext_ref[HALO - 2 + tap:HALO - 2 + tap + tb, :]
            dst[...] = _silu(acc).astype(dst.dtype)

    dt = _softplus(dt_ref[...].astype(F32) + dtb_ref[...])
    dts_ref[...] = dt
    as_ref[...] = dt * acoef_ref[...]

    nc = tb // chunk
    head0 = SSD_HEADS if reverse else 0
    rr = lax.broadcasted_iota(jnp.int32, (chunk, chunk), 0)
    cc = lax.broadcasted_iota(jnp.int32, (chunk, chunk), 1)
    valid = (rr <= cc) if reverse else (rr >= cc)
    lane = lax.broadcasted_iota(jnp.int32, (chunk, LANES), 1)
    low = lane < SSD_HEAD_DIM
    end_row = 0 if reverse else chunk - 1

    def body(ci, carry):
        c = (nc - 1 - ci) if reverse else ci
        r0 = pl.multiple_of(c * chunk, chunk)
        rows = pl.ds(r0, chunk)
        dtc = dts_ref[rows, :]
        acs = _sel_dot(tri_ref[...], as_ref[rows, :])
        a_end = acs[end_row:end_row + 1, :]
        acs_t = jnp.transpose(acs)
        narrow = jnp.concatenate([dtc, jnp.exp(acs), jnp.exp(a_end - acs)], axis=0)
        hi = narrow.astype(BF16)
        lo = (narrow - hi.astype(F32)).astype(BF16)
        wide = _dot(hi, expand_ref[...]) + _dot(lo, expand_ref[...])
        dt_x = wide[0:chunk]
        e_acs = wide[chunk:2 * chunk]
        e_rest = wide[2 * chunk:3 * chunk]

        shared = []
        for g in range(SSD_GROUPS):
            bg = ubc_ref[rows, g * LANES:(g + 1) * LANES]
            cg = ubc_ref[rows, (SSD_GROUPS + g) * LANES:(SSD_GROUPS + g + 1) * LANES]
            st = st_ref[g]
            shared.append((bg, st, _dot_nt(cg, bg), _dot(cg, st.astype(BF16))))
        for g in range(SSD_GROUPS):
            bg, st, gm, c_st = shared[g]
            wx, e_end = [], []
            for jp in range(2):
                pair = 2 * g + jp
                pcols = slice(pair * LANES, (pair + 1) * LANES)
                ea = e_acs[:, pcols]
                xdt = ux_ref[rows, pcols] * dt_x[:, pcols]
                wx.append((e_rest[:, pcols] * xdt).astype(BF16))
                e_end.append(ea[end_row:end_row + 1, :])
                acc = c_st[:, jp * LANES:(jp + 1) * LANES] * ea
                for e in range(2):
                    hl = head0 + 2 * pair + e
                    dmat = acs[:, hl:hl + 1] - acs_t[hl:hl + 1, :]
                    lm = jnp.where(valid, jnp.exp(jnp.minimum(dmat, 0.0)), 0.0)
                    m = (gm * lm).astype(BF16)
                    xm = jnp.where(low if e == 0 else jnp.logical_not(low), xdt, 0.0).astype(BF16)
                    acc = acc + _dot(m, xm)
                if reverse:
                    ys_ref[rows, pcols] = acc
                else:
                    o_ref[rows, pcols] = acc
            st_ref[g] = (jnp.concatenate(e_end, axis=1) * st
                         + _dot_tn(bg, jnp.concatenate(wx, axis=1)))
        return carry

    lax.fori_loop(0, nc, body, 0, unroll=2)

    if reverse:
        y = ys_ref[...] + yf_ref[...] + dskip_ref[...] * ux_ref[...]
        y = y * _silu(z_ref[...].astype(F32))
        ms = jnp.mean(y * y, axis=-1, keepdims=True)
        o_ref[...] = (y * lax.rsqrt(ms + 1e-6) * gain_ref[...]).astype(o_ref.dtype)


def _ssd(h, reverse, params, fwd=None):
    bsz, seq, _ = h.shape
    tb = min(SCAN_TOKENS, seq)
    nb = seq // tb
    chunk = min(SSD_CHUNK, tb)
    conv_w, conv_b, dt_bias, a_coef, tri, expand, d_skip, gain = params
    hb = tb // HALO
    nhb = seq // HALO

    def blk(n):
        return (nb - 1 - n) if reverse else n

    def col(group, width=GW, scale=1):
        return pl.BlockSpec((None, tb, width), lambda b, n: (b, blk(n), group * scale))

    def prev(group):
        return pl.BlockSpec((None, HALO, GW), lambda b, n: (b, jnp.maximum(blk(n) * hb - 1, 0), group))

    def nxt(group):
        return pl.BlockSpec((None, HALO, GW), lambda b, n: (b, jnp.minimum((blk(n) + 1) * hb, nhb - 1), group))

    def full(arr):
        nd = arr.ndim
        return pl.BlockSpec(arr.shape, lambda b, n: (0,) * nd)

    plain = pl.BlockSpec((None, tb, GW), lambda b, n: (b, blk(n), 0))
    dt_spec = pl.BlockSpec((None, tb, LANES), lambda b, n: (b, blk(n), NARROW_DT))
    state = pltpu.VMEM((SSD_GROUPS, SSD_STATE, 2 * LANES), F32)
    narrow = pltpu.VMEM((tb, LANES), F32)
    kernel = functools.partial(_ssd_kernel, reverse=reverse, chunk=chunk, tb=tb, nb=nb)
    if not reverse:
        return pl.pallas_call(
            kernel,
            grid=(bsz, nb),
            in_specs=[col(G_SX), prev(G_SX), nxt(G_SX), col(G_SBC), prev(G_SBC), nxt(G_SBC),
                      full(conv_w), full(conv_b), dt_spec, full(dt_bias), full(a_coef), full(tri),
                      full(expand)],
            out_specs=[plain, plain, plain],
            out_shape=[jax.ShapeDtypeStruct((bsz, seq, GW), F32),
                       jax.ShapeDtypeStruct((bsz, seq, GW), F32),
                       jax.ShapeDtypeStruct((bsz, seq, GW), BF16)],
            scratch_shapes=[state, narrow, narrow, pltpu.VMEM((tb + 2 * HALO, GW), F32)],
            compiler_params=_params(("arbitrary", "arbitrary")),
            name="ssd_fwd",
        )(h, h, h, h, h, h, conv_w, conv_b, h, dt_bias, a_coef, tri, expand)
    y_fwd, u_x, u_bc = fwd
    return pl.pallas_call(
        kernel,
        grid=(bsz, nb),
        in_specs=[plain, plain, dt_spec, full(dt_bias), full(a_coef), full(tri), full(expand),
                  col(G_SZ), plain, full(d_skip), full(gain)],
        out_specs=plain,
        out_shape=jax.ShapeDtypeStruct((bsz, seq, GW), BF16),
        scratch_shapes=[state, narrow, narrow, pltpu.VMEM((tb, GW), F32)],
        compiler_params=_params(("arbitrary", "arbitrary")),
        name="ssd_bwd",
    )(u_x, u_bc, h, dt_bias, a_coef, tri, expand, h, y_fwd, d_skip, gain)


def _out_kernel(ya_ref, yh_ref, ys_ref, yg_ref, x_ref, w_ref, g_ref, b_ref, xo_ref, xb_ref):
    acc = _dot(ya_ref[...], w_ref[0:GW, :])
    acc = acc + _dot(yh_ref[...], w_ref[GW:2 * GW, :])
    acc = acc + _dot(ys_ref[...], w_ref[2 * GW:3 * GW, :])
    acc = acc + _dot(yg_ref[...], w_ref[3 * GW:4 * GW, :])
    r = DEEPNORM_ALPHA * x_ref[...] + acc
    mu = jnp.mean(r, axis=-1, keepdims=True)
    rc = r - mu
    var = jnp.mean(rc * rc, axis=-1, keepdims=True)
    y = rc * lax.rsqrt(var + 1e-5) * g_ref[...] + b_ref[...]
    xo_ref[...] = y
    xb_ref[...] = y.astype(BF16)


def _out_proj_ln(ya, yh, ys, yg, x, w, ln_g, ln_b):
    t, d = x.shape
    tm = min(512, t)
    row = lambda width: pl.BlockSpec((tm, width), lambda i: (i, 0))
    const = lambda arr: pl.BlockSpec(arr.shape, lambda i: (0, 0))
    return pl.pallas_call(
        _out_kernel,
        grid=(t // tm,),
        in_specs=[row(GW), row(GW), row(GW), row(GW), row(d), const(w), const(ln_g), const(ln_b)],
        out_specs=[row(d), row(d)],
        out_shape=[jax.ShapeDtypeStruct((t, d), F32), jax.ShapeDtypeStruct((t, d), BF16)],
        compiler_params=_params(("parallel",)),
        name="out_proj_ln",
    )(ya, yh, ys, yg, x, w, ln_g, ln_b)


def _pad_heads(w, heads, width):
    d = w.shape[0]
    w = w.reshape(d, heads, width)
    w = jnp.pad(w, ((0, 0), (0, 0), (0, LANES - width)))
    return w.reshape(d, heads * LANES)


def _layout_w_in(w):
    d = w.shape[0]
    sizes = (512, 128, 128, 512, 512, 512, 512, 512, 512, 1024, 8, 8, 512, 256, 256, 512, 16, 16, 512)
    offs = np.concatenate([[0], np.cumsum(sizes)])
    (a_q, a_k, a_v, a_z, h_q, h_ff, h_fb, h_i, h_z,
     s_xbc, s_dtf, s_dtb, s_z, g_q, g_k, g_v, g_lf, g_lb, g_z) = [
        w[:, int(offs[i]):int(offs[i + 1])] for i in range(len(sizes))]
    zeros = lambda n: jnp.zeros((d, n), w.dtype)
    groups = [
        a_q, a_z, jnp.concatenate([a_k, a_v, s_dtf, s_dtb, zeros(112), g_lf, g_lb, zeros(96)], axis=1),
        h_q, h_ff, h_fb, h_i, h_z,
        s_xbc[:, :512], s_xbc[:, 512:], s_z,
        _pad_heads(g_q, GLA_HEADS, GLA_HEAD_K), _pad_heads(g_k, GLA_HEADS, GLA_HEAD_K), g_v, g_z,
    ]
    return jnp.concatenate(groups, axis=1).astype(BF16)


def _rope_tables(seq):
    rows = seq // GRID_W
    row_pos = jnp.repeat(jnp.arange(rows, dtype=F32), GRID_W)
    col_pos = jnp.tile(jnp.arange(GRID_W, dtype=F32), rows)
    axis_dim = ATTN_HEAD_DIM // 2
    inv_freq = jnp.power(ROPE_THETA, -jnp.arange(0, axis_dim, 2, dtype=F32) / axis_dim)
    ang_r = row_pos[:, None] * inv_freq
    ang_c = col_pos[:, None] * inv_freq
    cos = jnp.concatenate([jnp.cos(ang_r)] * 2 + [jnp.cos(ang_c)] * 2, axis=1)
    sin = jnp.concatenate([-jnp.sin(ang_r), jnp.sin(ang_r), -jnp.sin(ang_c), jnp.sin(ang_c)], axis=1)
    return jnp.tile(cos, (1, 2)), jnp.tile(sin, (1, 2))


def _ssd_expand(reverse):
    m = np.zeros((LANES, GW), np.float32)
    for hd in range(SSD_HEADS):
        m[hd + (SSD_HEADS if reverse else 0), hd * SSD_HEAD_DIM:(hd + 1) * SSD_HEAD_DIM] = 1.0
    return jnp.asarray(m, BF16)


def _tri(chunk, reverse):
    t = np.arange(chunk)[:, None]
    u = np.arange(chunk)[None, :]
    return jnp.asarray((u >= t) if reverse else (u <= t), BF16)


def kernel(x, w_in, attn_q_norm, attn_k_norm, hgrn_lb_logits, hgrn_norm, ssd_conv_w, ssd_conv_b,
           ssd_dt_bias, ssd_a_log, ssd_d, ssd_norm, gla_gk_w2, gla_gk_b, gla_norm, w_out, ln_g, ln_b):
    bsz, seq, d = x.shape
    t = bsz * seq
    rope = _rope_tables(seq)
    blocksum = jnp.asarray(np.kron(np.eye(GW // ATTN_HEAD_DIM), np.ones((ATTN_HEAD_DIM, ATTN_HEAD_DIM))), BF16)
    lower_bounds = jnp.cumsum(jax.nn.softmax(hgrn_lb_logits.astype(F32), axis=0), axis=0)
    lower_bounds = lower_bounds - lower_bounds[0]
    scan_consts = {rev: _scan_constants(LIN_CHUNK, min(SCAN_PRE_ROWS, seq), rev) for rev in (False, True)}
    ssd_chunk = min(SSD_CHUNK, min(SCAN_TOKENS, seq))

    xf = x.reshape(t, d)
    xb = xf
    for i in range(DEPTH):
        h = _in_proj(xb, _layout_w_in(w_in[i])).reshape(bsz, seq, N_PROJ)

        y_a = _attention(h, rope, jnp.tile(attn_q_norm[i], ATTN_HEADS)[None, :].astype(F32),
                         jnp.tile(attn_k_norm[i], 2)[None, :].astype(F32), blocksum)

        lb = jnp.maximum(lower_bounds[i], 0.0)[None, :]
        hgrn_params = (lb, hgrn_norm[i][None, :].astype(F32))
        o_f = _gated_scan(h, "hgrn", False, scan_consts[False], hgrn_params)
        y_h = _gated_scan(h, "hgrn", True, scan_consts[True], hgrn_params, o_f)

        gla_out = None
        for rev in (False, True):
            w2 = jnp.zeros((LANES, GW), F32)
            w2 = w2.at[rev * GLA_GATE_RANK:(rev + 1) * GLA_GATE_RANK].set(
                _pad_heads(gla_gk_w2[i, int(rev)], GLA_HEADS, GLA_HEAD_K))
            gb = _pad_heads(gla_gk_b[i, int(rev)][None, :], GLA_HEADS, GLA_HEAD_K)
            gla_params = (w2.astype(BF16), gb.astype(F32), jnp.tile(gla_norm[i], GLA_HEADS)[None, :].astype(F32))
            gla_out = _gated_scan(h, "gla", rev, scan_consts[rev], gla_params, gla_out)
        y_g = gla_out

        dt_bias = jnp.zeros((1, LANES), F32).at[0, :2 * SSD_HEADS].set(ssd_dt_bias[i].astype(F32).reshape(-1))
        a_coef = jnp.zeros((1, LANES), F32).at[0, :2 * SSD_HEADS].set(-jnp.exp(ssd_a_log[i].astype(F32)).reshape(-1))
        ssd_out = None
        for rev in (False, True):
            ssd_params = (ssd_conv_w[i].astype(F32), ssd_conv_b[i][None, :].astype(F32), dt_bias, a_coef,
                          _tri(ssd_chunk, rev), _ssd_expand(rev), jnp.repeat(ssd_d[i].astype(F32), SSD_HEAD_DIM)[None, :],
                          ssd_norm[i][None, :].astype(F32))
            ssd_out = _ssd(h, rev, ssd_params, ssd_out)
        y_s = ssd_out

        flat = lambda a: a.reshape(t, GW)
        xf, xb = _out_proj_ln(flat(y_a), flat(y_h), flat(y_s), flat(y_g), xf, w_out[i].astype(BF16),
                              ln_g[i][None, :].astype(F32), ln_b[i][None, :].astype(F32))
    return xf.reshape(bsz, seq, d)
```

```python
import functools

import numpy as np
import jax
import jax.numpy as jnp
from jax import lax
from jax.experimental import pallas as pl
from jax.experimental.pallas import tpu as pltpu

F32 = jnp.float32
BF16 = jnp.bfloat16

D_MODEL = 1024
DEPTH = 2
GROUP_WIDTH = 512
GRID_W = 64
ATTN_HEADS = 8
ATTN_KV_HEADS = 2
ATTN_HEAD_DIM = 64
ROPE_THETA = 10000.0
HGRN_HEADS = 4
HGRN_EXPAND = 128
SSD_HEADS = 8
SSD_HEAD_DIM = 64
SSD_GROUPS = 2
SSD_STATE = 128
SSD_CONV_WIDTH = 5
GLA_HEADS = 4
GLA_HEAD_K = 64
GLA_HEAD_V = 128
GLA_GATE_RANK = 16
GLA_GATE_NORMALIZER = 16.0
DEEPNORM_ALPHA = (2 * DEPTH) ** 0.25

LANES = 128
SUBLANES = 8
HALO = 16
GW = GROUP_WIDTH

(G_AQ, G_AZ, G_AKV, G_HQ, G_HFF, G_HFB, G_HI, G_HZ,
 G_SX, G_SBC, G_SZ, G_GQ, G_GK, G_GV, G_GZ) = range(15)
N_GROUPS = 15
NARROW_DT = G_AKV * 4 + 2
NARROW_LOW = G_AKV * 4 + 3
N_PROJ = N_GROUPS * GW

LIN_CHUNK = 64
LIN_SUB = 4
SSD_CHUNK = 128
SCAN_TOKENS = 512
SCAN_PRE_ROWS = 128
ATTN_TQ = 256
ATTN_KEY_BLOCKS = 2
VMEM_LIMIT = 48 * 1024 * 1024


def _dot(a, b):
    return jnp.dot(a, b, preferred_element_type=F32)


def _dot_nt(a, b):
    return lax.dot_general(a, b, (((1,), (1,)), ((), ())), preferred_element_type=F32)


def _dot_tn(a, b):
    return lax.dot_general(a, b, (((0,), (0,)), ((), ())), preferred_element_type=F32)


def _split3(x):
    hi = x.astype(BF16)
    r1 = x - hi.astype(F32)
    mid = r1.astype(BF16)
    lo = (r1 - mid.astype(F32)).astype(BF16)
    return hi, mid, lo


def _sel_dot(sel, x):
    hi, mid, lo = _split3(x)
    return _dot(sel, hi) + _dot(sel, mid) + _dot(sel, lo)


def _log_sigmoid(x):
    return jnp.minimum(x, 0.0) - jnp.log1p(jnp.exp(-jnp.abs(x)))


def _silu(x):
    return x * jax.nn.sigmoid(x)


def _softplus(x):
    return jnp.maximum(x, 0.0) + jnp.log1p(jnp.exp(-jnp.abs(x)))


def _params(sem):
    return pltpu.CompilerParams(dimension_semantics=sem, vmem_limit_bytes=VMEM_LIMIT)


def _matmul_kernel(x_ref, w_ref, o_ref):
    o_ref[...] = _dot(x_ref[...].astype(BF16), w_ref[...]).astype(o_ref.dtype)


def _in_proj(xb, w):
    t, d = xb.shape
    n = w.shape[1]
    tm = min(2048, t)
    tn = GW
    return pl.pallas_call(
        _matmul_kernel,
        grid=(t // tm, n // tn),
        in_specs=[pl.BlockSpec((tm, d), lambda i, j: (i, 0)),
                  pl.BlockSpec((d, tn), lambda i, j: (0, j))],
        out_specs=pl.BlockSpec((tm, tn), lambda i, j: (i, j)),
        out_shape=jax.ShapeDtypeStruct((t, n), BF16),
        compiler_params=_params(("parallel", "parallel")),
        name="in_proj",
    )(xb, w)


def _head_rms(x, blocksum, gain, eps=1e-6):
    xx = x * x
    hi = xx.astype(BF16)
    lo = (xx - hi.astype(F32)).astype(BF16)
    ssq = _dot(hi, blocksum) + _dot(lo, blocksum)
    return x * lax.rsqrt(ssq * (1.0 / ATTN_HEAD_DIM) + eps) * gain


def _rope(x, cos, sin_signed):
    w = x.shape[1]
    lane = lax.broadcasted_iota(jnp.int32, x.shape, 1)
    first = (lane & 31) < 16
    partner = jnp.where(first, pltpu.roll(x, w - 16, 1), pltpu.roll(x, 16, 1))
    return x * cos + partner * sin_signed


def _attn_kernel(q_ref, z_ref, kv_ref, cos_ref, sin_ref, cosk_ref, sink_ref, qg_ref, kg_ref,
                 bs_ref, o_ref, kd_ref, vd_ref, *, seq, prep_rows):
    n = pl.program_id(1)

    @pl.when(n == 0)
    def _prep_kv():
        def body(i, carry):
            r0 = pl.multiple_of(i * prep_rows, prep_rows)
            kv = kv_ref[pl.ds(r0, prep_rows), :].astype(F32)
            k = kv[:, :LANES]
            v = kv[:, LANES:]
            k = _head_rms(k, bs_ref[0:LANES, 0:LANES], kg_ref[...])
            k = _rope(k, cosk_ref[pl.ds(r0, prep_rows), :], sink_ref[pl.ds(r0, prep_rows), :])
            lane = lax.broadcasted_iota(jnp.int32, k.shape, 1)
            low = lane < ATTN_HEAD_DIM
            k_sw = pltpu.roll(k, ATTN_HEAD_DIM, 1)
            v_sw = pltpu.roll(v, ATTN_HEAD_DIM, 1)
            kd_ref[0, pl.ds(r0, prep_rows), :] = jnp.where(low, k, k_sw).astype(BF16)
            kd_ref[1, pl.ds(r0, prep_rows), :] = jnp.where(low, k_sw, k).astype(BF16)
            vd_ref[0, pl.ds(r0, prep_rows), :] = jnp.where(low, v, v_sw).astype(BF16)
            vd_ref[1, pl.ds(r0, prep_rows), :] = jnp.where(low, v_sw, v).astype(BF16)
            return carry
        lax.fori_loop(0, seq // prep_rows, body, 0)

    q = _head_rms(q_ref[...].astype(F32), bs_ref[...], qg_ref[...])
    cos = jnp.concatenate([cos_ref[...]] * 4, axis=1)
    sin = jnp.concatenate([sin_ref[...]] * 4, axis=1)
    q = _rope(q, cos, sin) * (ATTN_HEAD_DIM ** -0.5)
    lane = lax.broadcasted_iota(jnp.int32, (q.shape[0], LANES), 1)
    low = lane < ATTN_HEAD_DIM
    group = ATTN_HEADS // ATTN_KV_HEADS
    kb_len = seq // ATTN_KEY_BLOCKS

    q_heads = []
    for hd in range(ATTN_HEADS):
        pair, e = divmod(hd, 2)
        qp = q[:, pair * LANES:(pair + 1) * LANES]
        q_heads.append(jnp.where(low if e == 0 else jnp.logical_not(low), qp, 0.0).astype(BF16))

    def scores(item):
        hd, kb = divmod(item, ATTN_KEY_BLOCKS)
        return _dot_nt(q_heads[hd], kd_ref[hd // group, kb * kb_len:(kb + 1) * kb_len, :])

    outs = []
    n_items = ATTN_HEADS * ATTN_KEY_BLOCKS
    s_next = scores(0)
    m_run = l_run = o_run = None
    for item in range(n_items):
        hd, kb = divmod(item, ATTN_KEY_BLOCKS)
        s = s_next
        if item + 1 < n_items:
            s_next = scores(item + 1)
        v_blk = vd_ref[hd // group, kb * kb_len:(kb + 1) * kb_len, :]
        m_blk = jnp.max(s, axis=-1, keepdims=True)
        if kb == 0:
            m_run = m_blk
            p = jnp.exp(s - m_run)
            l_run = jnp.sum(p, axis=-1, keepdims=True)
            o_run = _dot(p.astype(BF16), v_blk)
        else:
            m_new = jnp.maximum(m_run, m_blk)
            alpha = jnp.exp(m_run - m_new)
            p = jnp.exp(s - m_new)
            l_run = alpha * l_run + jnp.sum(p, axis=-1, keepdims=True)
            o_run = alpha * o_run + _dot(p.astype(BF16), v_blk)
            m_run = m_new
        if kb == ATTN_KEY_BLOCKS - 1:
            outs.append(o_run / l_run)
    for pair in range(ATTN_HEADS // 2):
        cols = slice(pair * LANES, (pair + 1) * LANES)
        o_pair = jnp.where(low, outs[2 * pair], outs[2 * pair + 1])
        o_ref[:, cols] = (o_pair * _silu(z_ref[:, cols].astype(F32))).astype(o_ref.dtype)


def _attention(h, rope, q_gain, k_gain, blocksum):
    bsz, seq, _ = h.shape
    tq = min(ATTN_TQ, seq)
    prep_rows = min(512, seq)
    cos2, sin2 = rope
    kernel = functools.partial(_attn_kernel, seq=seq, prep_rows=prep_rows)
    return pl.pallas_call(
        kernel,
        grid=(bsz, seq // tq),
        in_specs=[
            pl.BlockSpec((None, tq, GW), lambda b, n: (b, n, G_AQ)),
            pl.BlockSpec((None, tq, GW), lambda b, n: (b, n, G_AZ)),
            pl.BlockSpec((None, seq, 2 * LANES), lambda b, n: (b, 0, G_AKV * 2)),
            pl.BlockSpec((tq, LANES), lambda b, n: (n, 0)),
            pl.BlockSpec((tq, LANES), lambda b, n: (n, 0)),
            pl.BlockSpec((seq, LANES), lambda b, n: (0, 0)),
            pl.BlockSpec((seq, LANES), lambda b, n: (0, 0)),
            pl.BlockSpec((1, GW), lambda b, n: (0, 0)),
            pl.BlockSpec((1, LANES), lambda b, n: (0, 0)),
            pl.BlockSpec((GW, GW), lambda b, n: (0, 0)),
        ],
        out_specs=pl.BlockSpec((None, tq, GW), lambda b, n: (b, n, 0)),
        out_shape=jax.ShapeDtypeStruct((bsz, seq, GW), BF16),
        scratch_shapes=[pltpu.VMEM((2, seq, LANES), BF16), pltpu.VMEM((2, seq, LANES), BF16)],
        compiler_params=_params(("arbitrary", "arbitrary")),
        name="attention",
    )(h, h, h, cos2, sin2, cos2, sin2, q_gain, k_gain, blocksum)


def _scan_constants(chunk, tb, reverse):
    t = np.arange(chunk)[:, None]
    u = np.arange(chunk)[None, :]
    tri = (u >= t) if reverse else (u <= t)
    q_half = 0 if reverse else 1
    masks = []
    levels = []
    h = chunk // 2
    while h >= LIN_SUB:
        same_block = (t // (2 * h)) == (u // (2 * h))
        masks.append(same_block & ((t // h) % 2 == q_half) & ((u // h) % 2 == 1 - q_half))
        levels.append((h, q_half))
        h //= 2
    masks = np.stack(masks, axis=0).astype(np.float32)
    tri_blocks = np.kron(np.eye(tb // chunk), tri)
    return jnp.asarray(tri_blocks, BF16), jnp.asarray(masks, F32), tuple(levels)


def _scan_kernel(*refs, mode, reverse, final, chunk, tb, levels, pre_rows):
    it = iter(refs)
    if mode == "hgrn":
        q_ref, f_ref, v_ref = next(it), next(it), next(it)
        lb_ref = next(it)
    else:
        q_ref, k_ref, v_ref, low_ref = next(it), next(it), next(it), next(it)
        w2_ref, gb_ref = next(it), next(it)
    if final:
        z_ref, of_ref, gain_ref = next(it), next(it), next(it)
    stack_ref, lmask_ref = next(it), next(it)
    o_ref = next(it)
    st_ref, qs_ref, ks_ref, gs_ref, bs_ref = (next(it) for _ in range(5))
    if final:
        os_ref = next(it)

    n = pl.program_id(1)

    @pl.when(n == 0)
    def _reset():
        st_ref[...] = jnp.zeros_like(st_ref)

    if mode == "gla":
        gs_ref[...] = _dot(low_ref[...], w2_ref[...]) + gb_ref[...]
    for r in range(0, tb, pre_rows):
        rows = slice(r, r + pre_rows)
        if mode == "hgrn":
            qr = q_ref[rows, :].astype(F32)
            qs_ref[rows, :] = _silu(qr) * (HGRN_EXPAND ** -0.5)
            f = f_ref[rows, :].astype(F32)
            e = jnp.exp(-jnp.abs(f))
            r = 1.0 / (1.0 + e)
            pos = f >= 0.0
            sig = jnp.where(pos, r, e * r)
            sig_neg = jnp.where(pos, e * r, r)
            lb = lb_ref[...]
            om = 1.0 - lb
            log_sig = jnp.minimum(f, 0.0) + jnp.log(r)
            g_rows = jnp.where(lb > 0.0, jnp.log(lb + om * sig), jnp.log(om) + log_sig)
            ks_ref[rows, :] = om * sig_neg
        else:
            qs_ref[rows, :] = q_ref[rows, :].astype(F32) * (GLA_HEAD_K ** -0.5)
            ks_ref[rows, :] = k_ref[rows, :].astype(F32)
            g_rows = _log_sigmoid(gs_ref[rows, :]) * (1.0 / GLA_GATE_NORMALIZER)
        gs_ref[rows, :] = g_rows
        bs_ref[rows, :] = _sel_dot(stack_ref[...], g_rows)

    nc = tb // chunk
    shift = (chunk - 1) if reverse else 1
    row = lax.broadcasted_iota(jnp.int32, (chunk, LANES), 0)
    rr = lax.broadcasted_iota(jnp.int32, (chunk, chunk), 0)
    cc = lax.broadcasted_iota(jnp.int32, (chunk, chunk), 1)
    diff = (cc - rr) if reverse else (rr - cc)
    sub_edge = (LIN_SUB - 1) if reverse else 0
    keep = (row & (LIN_SUB - 1)) != sub_edge
    end_row = 0 if reverse else chunk - 1

    def body(ci, carry):
        c = (nc - 1 - ci) if reverse else ci
        r0 = pl.multiple_of(c * chunk, chunk)
        heads = range(4)
        lanes = [slice(hh * LANES, (hh + 1) * LANES) for hh in heads]
        q = [qs_ref[pl.ds(r0, chunk), ln] for ln in lanes]
        k = [ks_ref[pl.ds(r0, chunk), ln] for ln in lanes]
        g = [gs_ref[pl.ds(r0, chunk), ln] for ln in lanes]
        v = [v_ref[pl.ds(r0, chunk), ln] for ln in lanes]
        b = [bs_ref[pl.ds(r0, chunk), ln] for ln in lanes]

        o = []
        for hh in heads:
            st = st_ref[hh]
            total = b[hh][end_row:end_row + 1, :]
            o.append(_dot_nt((q[hh] * jnp.exp(b[hh])).astype(BF16), st.astype(BF16)))
            k_end = (k[hh] * jnp.exp(total - b[hh])).astype(BF16)
            st_ref[hh] = jnp.exp(total) * st + _dot_tn(v[hh], k_end)

        scores = [jnp.zeros((chunk, chunk), F32) for _ in heads]
        for li, (h, q_half) in enumerate(levels):
            is_q = ((row // h) & 1) == q_half
            ref_row = h - 1 if reverse else h
            for hh in heads:
                b3 = b[hh].reshape(chunk // (2 * h), 2 * h, LANES)
                d3 = b3 - b3[:, ref_row:ref_row + 1, :]
                log_e = -jnp.abs(d3.reshape(chunk, LANES))
                x = (jnp.where(is_q, q[hh], k[hh]) * jnp.exp(log_e)).astype(BF16)
                scores[hh] = scores[hh] + _dot_nt(x, x) * lmask_ref[li]

        for hh in heads:
            f0 = jnp.where(keep, jnp.exp(g[hh]), 0.0)
            a = k[hh]
            band = jnp.where(diff == 0, jnp.sum(q[hh] * a, axis=-1, keepdims=True), 0.0)
            for d in range(1, LIN_SUB):
                a = f0 * pltpu.roll(a, shift, 0)
                band = jnp.where(diff == d, jnp.sum(q[hh] * a, axis=-1, keepdims=True), band)
            scores[hh] = scores[hh] + band

        for hh in heads:
            out = o[hh] + _dot(scores[hh].astype(BF16), v[hh])
            if final:
                os_ref[pl.ds(r0, chunk), lanes[hh]] = out
            else:
                o_ref[pl.ds(r0, chunk), lanes[hh]] = out
        return carry

    lax.fori_loop(0, nc, body, 0, unroll=4)

    if final:
        o = os_ref[...] + of_ref[...]
        if mode == "hgrn":
            ms = jnp.mean(o * o, axis=-1, keepdims=True)
            y = o * lax.rsqrt(ms + 1e-6) * gain_ref[...]
        else:
            parts = []
            for hh in range(4):
                oh = o[:, hh * LANES:(hh + 1) * LANES]
                ms = jnp.mean(oh * oh, axis=-1, keepdims=True)
                parts.append(oh * lax.rsqrt(ms + 1e-6))
            y = jnp.concatenate(parts, axis=1) * gain_ref[...]
        o_ref[...] = (y * _silu(z_ref[...].astype(F32))).astype(o_ref.dtype)


def _gated_scan(h, mode, reverse, consts, params, o_fwd=None):
    bsz, seq, _ = h.shape
    tb = min(SCAN_TOKENS, seq)
    nb = seq // tb
    chunk = LIN_CHUNK
    stack, lmask, levels = consts
    final = reverse

    def blk(n):
        return (nb - 1 - n) if reverse else n

    def col(group):
        return pl.BlockSpec((None, tb, GW), lambda b, n: (b, blk(n), group))

    def full(arr):
        nd = arr.ndim
        return pl.BlockSpec(arr.shape, lambda b, n: (0,) * nd)

    if mode == "hgrn":
        lower_bound, gain = params
        ins = [h, h, h, lower_bound]
        specs = [col(G_HQ), col(G_HFB if reverse else G_HFF), col(G_HI), full(lower_bound)]
        z_group = G_HZ
    else:
        w2, gb, gain = params
        ins = [h, h, h, h, w2, gb]
        specs = [col(G_GQ), col(G_GK), col(G_GV),
                 pl.BlockSpec((None, tb, LANES), lambda b, n: (b, blk(n), NARROW_LOW)),
                 full(w2), full(gb)]
        z_group = G_GZ
    if final:
        ins += [h, o_fwd, gain]
        specs += [col(z_group), pl.BlockSpec((None, tb, GW), lambda b, n: (b, blk(n), 0)), full(gain)]
    ins += [stack, lmask]
    specs += [full(stack), full(lmask)]
    scratch = [pltpu.VMEM((4, LANES, LANES), F32)] + [pltpu.VMEM((tb, GW), F32)] * 4
    if final:
        scratch.append(pltpu.VMEM((tb, GW), F32))
    kernel = functools.partial(_scan_kernel, mode=mode, reverse=reverse, final=final,
                               chunk=chunk, tb=tb, levels=levels, pre_rows=stack.shape[0])
    return pl.pallas_call(
        kernel,
        grid=(bsz, nb),
        in_specs=specs,
        out_specs=pl.BlockSpec((None, tb, GW), lambda b, n: (b, blk(n), 0)),
        out_shape=jax.ShapeDtypeStruct((bsz, seq, GW), BF16 if final else F32),
        scratch_shapes=scratch,
        compiler_params=_params(("arbitrary", "arbitrary")),
        name=f"{mode}_{'bwd' if reverse else 'fwd'}",
    )(*ins)


def _ssd_kernel(*refs, reverse, chunk, tb, nb):
    it = iter(refs)
    if not reverse:
        x_ref, xp_ref, xn_ref, bc_ref, bcp_ref, bcn_ref, cw_ref, cb_ref = (next(it) for _ in range(8))
    else:
        ux_ref, ubc_ref = next(it), next(it)
    dt_ref, dtb_ref, acoef_ref, tri_ref, expand_ref = (next(it) for _ in range(5))
    if reverse:
        z_ref, yf_ref, dskip_ref, gain_ref = (next(it) for _ in range(4))
        o_ref = next(it)
    else:
        o_ref, ux_ref, ubc_ref = next(it), next(it), next(it)
    st_ref, dts_ref, as_ref = next(it), next(it), next(it)
    if reverse:
        ys_ref = next(it)
    else:
        ext_ref = next(it)

    n = pl.program_id(1)
    pos = (nb - 1 - n) if reverse else n

    @pl.when(n == 0)
    def _reset():
        st_ref[...] = jnp.zeros_like(st_ref)

    if not reverse:
        halves = ((x_ref, xp_ref, xn_ref, ux_ref), (bc_ref, bcp_ref, bcn_ref, ubc_ref))
        for hi, (cur, prev, nxt, dst) in enumerate(halves):
            cols = slice(hi * GW, (hi + 1) * GW)
            ext_ref[0:HALO, :] = jnp.where(pos > 0, prev[...].astype(F32), 0.0)
            ext_ref[HALO:HALO + tb, :] = cur[...].astype(F32)
            ext_ref[HALO + tb:2 * HALO + tb, :] = jnp.where(pos < nb - 1, nxt[...].astype(F32), 0.0)
            acc = jnp.broadcast_to(cb_ref[:, cols], (tb, GW))
            for tap in range(SSD_CONV_WIDTH):
                acc = acc + cw_ref[tap:tap + 1, cols] * ext_ref[HALO - 2 + tap:HALO - 2 + tap + tb, :]
            dst[...] = _silu(acc).astype(dst.dtype)

    dt = _softplus(dt_ref[...].astype(F32) + dtb_ref[...])
    dts_ref[...] = dt
    as_ref[...] = dt * acoef_ref[...]

    nc = tb // chunk
    head0 = SSD_HEADS if reverse else 0
    rr = lax.broadcasted_iota(jnp.int32, (chunk, chunk), 0)
    cc = lax.broadcasted_iota(jnp.int32, (chunk, chunk), 1)
    valid = (rr <= cc) if reverse else (rr >= cc)
    lane = lax.broadcasted_iota(jnp.int32, (chunk, LANES), 1)
    low = lane < SSD_HEAD_DIM
    end_row = 0 if reverse else chunk - 1

    def body(ci, carry):
        c = (nc - 1 - ci) if reverse else ci
        r0 = pl.multiple_of(c * chunk, chunk)
        rows = pl.ds(r0, chunk)
        dtc = dts_ref[rows, :]
        acs = _sel_dot(tri_ref[...], as_ref[rows, :])
        a_end = acs[end_row:end_row + 1, :]
        acs_t = jnp.transpose(acs)
        narrow = jnp.concatenate([dtc, jnp.exp(acs), jnp.exp(a_end - acs)], axis=0)
        hi = narrow.astype(BF16)
        lo = (narrow - hi.astype(F32)).astype(BF16)
        wide = _dot(hi, expand_ref[...]) + _dot(lo, expand_ref[...])
        dt_x = wide[0:chunk]
        e_acs = wide[chunk:2 * chunk]
        e_rest = wide[2 * chunk:3 * chunk]

        shared = []
        for g in range(SSD_GROUPS):
            bg = ubc_ref[rows, g * LANES:(g + 1) * LANES]
            cg = ubc_ref[rows, (SSD_GROUPS + g) * LANES:(SSD_GROUPS + g + 1) * LANES]
            st = st_ref[g]
            shared.append((bg, st, _dot_nt(cg, bg), _dot(cg, st.astype(BF16))))
        for g in range(SSD_GROUPS):
            bg, st, gm, c_st = shared[g]
            wx, e_end = [], []
            for jp in range(2):
                pair = 2 * g + jp
                pcols = slice(pair * LANES, (pair + 1) * LANES)
                ea = e_acs[:, pcols]
                xdt = ux_ref[rows, pcols] * dt_x[:, pcols]
                wx.append((e_rest[:, pcols] * xdt).astype(BF16))
                e_end.append(ea[end_row:end_row + 1, :])
                acc = c_st[:, jp * LANES:(jp + 1) * LANES] * ea
                for e in range(2):
                    hl = head0 + 2 * pair + e
                    dmat = acs[:, hl:hl + 1] - acs_t[hl:hl + 1, :]
                    lm = jnp.where(valid, jnp.exp(jnp.minimum(dmat, 0.0)), 0.0)
                    m = (gm * lm).astype(BF16)
                    xm = jnp.where(low if e == 0 else jnp.logical_not(low), xdt, 0.0).astype(BF16)
                    acc = acc + _dot(m, xm)
                if reverse:
                    ys_ref[rows, pcols] = acc
                else:
                    o_ref[rows, pcols] = acc
            st_ref[g] = (jnp.concatenate(e_end, axis=1) * st
                         + _dot_tn(bg, jnp.concatenate(wx, axis=1)))
        return carry

    lax.fori_loop(0, nc, body, 0, unroll=2)

    if reverse:
        y = ys_ref[...] + yf_ref[...] + dskip_ref[...] * ux_ref[...]
        y = y * _silu(z_ref[...].astype(F32))
        ms = jnp.mean(y * y, axis=-1, keepdims=True)
        o_ref[...] = (y * lax.rsqrt(ms + 1e-6) * gain_ref[...]).astype(o_ref.dtype)


def _ssd(h, reverse, params, fwd=None):
    bsz, seq, _ = h.shape
    tb = min(SCAN_TOKENS, seq)
    nb = seq // tb
    chunk = min(SSD_CHUNK, tb)
    conv_w, conv_b, dt_bias, a_coef, tri, expand, d_skip, gain = params
    hb = tb // HALO
    nhb = seq // HALO

    def blk(n):
        return (nb - 1 - n) if reverse else n

    def col(group, width=GW, scale=1):
        return pl.BlockSpec((None, tb, width), lambda b, n: (b, blk(n), group * scale))

    def prev(group):
        return pl.BlockSpec((None, HALO, GW), lambda b, n: (b, jnp.maximum(blk(n) * hb - 1, 0), group))

    def nxt(group):
        return pl.BlockSpec((None, HALO, GW), lambda b, n: (b, jnp.minimum((blk(n) + 1) * hb, nhb - 1), group))

    def full(arr):
        nd = arr.ndim
        return pl.BlockSpec(arr.shape, lambda b, n: (0,) * nd)

    plain = pl.BlockSpec((None, tb, GW), lambda b, n: (b, blk(n), 0))
    dt_spec = pl.BlockSpec((None, tb, LANES), lambda b, n: (b, blk(n), NARROW_DT))
    state = pltpu.VMEM((SSD_GROUPS, SSD_STATE, 2 * LANES), F32)
    narrow = pltpu.VMEM((tb, LANES), F32)
    kernel = functools.partial(_ssd_kernel, reverse=reverse, chunk=chunk, tb=tb, nb=nb)
    if not reverse:
        return pl.pallas_call(
            kernel,
            grid=(bsz, nb),
            in_specs=[col(G_SX), prev(G_SX), nxt(G_SX), col(G_SBC), prev(G_SBC), nxt(G_SBC),
                      full(conv_w), full(conv_b), dt_spec, full(dt_bias), full(a_coef), full(tri),
                      full(expand)],
            out_specs=[plain, plain, plain],
            out_shape=[jax.ShapeDtypeStruct((bsz, seq, GW), F32),
                       jax.ShapeDtypeStruct((bsz, seq, GW), F32),
                       jax.ShapeDtypeStruct((bsz, seq, GW), BF16)],
            scratch_shapes=[state, narrow, narrow, pltpu.VMEM((tb + 2 * HALO, GW), F32)],
            compiler_params=_params(("arbitrary", "arbitrary")),
            name="ssd_fwd",
        )(h, h, h, h, h, h, conv_w, conv_b, h, dt_bias, a_coef, tri, expand)
    y_fwd, u_x, u_bc = fwd
    return pl.pallas_call(
        kernel,
        grid=(bsz, nb),
        in_specs=[plain, plain, dt_spec, full(dt_bias), full(a_coef), full(tri), full(expand),
                  col(G_SZ), plain, full(d_skip), full(gain)],
        out_specs=plain,
        out_shape=jax.ShapeDtypeStruct((bsz, seq, GW), BF16),
        scratch_shapes=[state, narrow, narrow, pltpu.VMEM((tb, GW), F32)],
        compiler_params=_params(("arbitrary", "arbitrary")),
        name="ssd_bwd",
    )(u_x, u_bc, h, dt_bias, a_coef, tri, expand, h, y_fwd, d_skip, gain)


def _out_kernel(ya_ref, yh_ref, ys_ref, yg_ref, x_ref, w_ref, g_ref, b_ref, xo_ref, xb_ref):
    acc = _dot(ya_ref[...], w_ref[0:GW, :])
    acc = acc + _dot(yh_ref[...], w_ref[GW:2 * GW, :])
    acc = acc + _dot(ys_ref[...], w_ref[2 * GW:3 * GW, :])
    acc = acc + _dot(yg_ref[...], w_ref[3 * GW:4 * GW, :])
    r = DEEPNORM_ALPHA * x_ref[...] + acc
    mu = jnp.mean(r, axis=-1, keepdims=True)
    rc = r - mu
    var = jnp.mean(rc * rc, axis=-1, keepdims=True)
    y = rc * lax.rsqrt(var + 1e-5) * g_ref[...] + b_ref[...]
    xo_ref[...] = y
    xb_ref[...] = y.astype(BF16)


def _out_proj_ln(ya, yh, ys, yg, x, w, ln_g, ln_b):
    t, d = x.shape
    tm = min(512, t)
    row = lambda width: pl.BlockSpec((tm, width), lambda i: (i, 0))
    const = lambda arr: pl.BlockSpec(arr.shape, lambda i: (0, 0))
    return pl.pallas_call(
        _out_kernel,
        grid=(t // tm,),
        in_specs=[row(GW), row(GW), row(GW), row(GW), row(d), const(w), const(ln_g), const(ln_b)],
        out_specs=[row(d), row(d)],
        out_shape=[jax.ShapeDtypeStruct((t, d), F32), jax.ShapeDtypeStruct((t, d), BF16)],
        compiler_params=_params(("parallel",)),
        name="out_proj_ln",
    )(ya, yh, ys, yg, x, w, ln_g, ln_b)


def _pad_heads(w, heads, width):
    d = w.shape[0]
    w = w.reshape(d, heads, width)
    w = jnp.pad(w, ((0, 0), (0, 0), (0, LANES - width)))
    return w.reshape(d, heads * LANES)


def _layout_w_in(w):
    d = w.shape[0]
    sizes = (512, 128, 128, 512, 512, 512, 512, 512, 512, 1024, 8, 8, 512, 256, 256, 512, 16, 16, 512)
    offs = np.concatenate([[0], np.cumsum(sizes)])
    (a_q, a_k, a_v, a_z, h_q, h_ff, h_fb, h_i, h_z,
     s_xbc, s_dtf, s_dtb, s_z, g_q, g_k, g_v, g_lf, g_lb, g_z) = [
        w[:, int(offs[i]):int(offs[i + 1])] for i in range(len(sizes))]
    zeros = lambda n: jnp.zeros((d, n), w.dtype)
    groups = [
        a_q, a_z, jnp.concatenate([a_k, a_v, s_dtf, s_dtb, zeros(112), g_lf, g_lb, zeros(96)], axis=1),
        h_q, h_ff, h_fb, h_i, h_z,
        s_xbc[:, :512], s_xbc[:, 512:], s_z,
        _pad_heads(g_q, GLA_HEADS, GLA_HEAD_K), _pad_heads(g_k, GLA_HEADS, GLA_HEAD_K), g_v, g_z,
    ]
    return jnp.concatenate(groups, axis=1).astype(BF16)


def _rope_tables(seq):
    rows = seq // GRID_W
    row_pos = jnp.repeat(jnp.arange(rows, dtype=F32), GRID_W)
    col_pos = jnp.tile(jnp.arange(GRID_W, dtype=F32), rows)
    axis_dim = ATTN_HEAD_DIM // 2
    inv_freq = jnp.power(ROPE_THETA, -jnp.arange(0, axis_dim, 2, dtype=F32) / axis_dim)
    ang_r = row_pos[:, None] * inv_freq
    ang_c = col_pos[:, None] * inv_freq
    cos = jnp.concatenate([jnp.cos(ang_r)] * 2 + [jnp.cos(ang_c)] * 2, axis=1)
    sin = jnp.concatenate([-jnp.sin(ang_r), jnp.sin(ang_r), -jnp.sin(ang_c), jnp.sin(ang_c)], axis=1)
    return jnp.tile(cos, (1, 2)), jnp.tile(sin, (1, 2))


def _ssd_expand(reverse):
    m = np.zeros((LANES, GW), np.float32)
    for hd in range(SSD_HEADS):
        m[hd + (SSD_HEADS if reverse else 0), hd * SSD_HEAD_DIM:(hd + 1) * SSD_HEAD_DIM] = 1.0
    return jnp.asarray(m, BF16)


def _tri(chunk, reverse):
    t = np.arange(chunk)[:, None]
    u = np.arange(chunk)[None, :]
    return jnp.asarray((u >= t) if reverse else (u <= t), BF16)


def kernel(x, w_in, attn_q_norm, attn_k_norm, hgrn_lb_logits, hgrn_norm, ssd_conv_w, ssd_conv_b,
           ssd_dt_bias, ssd_a_log, ssd_d, ssd_norm, gla_gk_w2, gla_gk_b, gla_norm, w_out, ln_g, ln_b):
    bsz, seq, d = x.shape
    t = bsz * seq
    rope = _rope_tables(seq)
    blocksum = jnp.asarray(np.kron(np.eye(GW // ATTN_HEAD_DIM), np.ones((ATTN_HEAD_DIM, ATTN_HEAD_DIM))), BF16)
    lower_bounds = jnp.cumsum(jax.nn.softmax(hgrn_lb_logits.astype(F32), axis=0), axis=0)
    lower_bounds = lower_bounds - lower_bounds[0]
    scan_consts = {rev: _scan_constants(LIN_CHUNK, min(SCAN_PRE_ROWS, seq), rev) for rev in (False, True)}
    ssd_chunk = min(SSD_CHUNK, min(SCAN_TOKENS, seq))

    xf = x.reshape(t, d)
    xb = xf
    for i in range(DEPTH):
        h = _in_proj(xb, _layout_w_in(w_in[i])).reshape(bsz, seq, N_PROJ)

        y_a = _attention(h, rope, jnp.tile(attn_q_norm[i], ATTN_HEADS)[None, :].astype(F32),
                         jnp.tile(attn_k_norm[i], 2)[None, :].astype(F32), blocksum)

        lb = jnp.maximum(lower_bounds[i], 0.0)[None, :]
        hgrn_params = (lb, hgrn_norm[i][None, :].astype(F32))
        o_f = _gated_scan(h, "hgrn", False, scan_consts[False], hgrn_params)
        y_h = _gated_scan(h, "hgrn", True, scan_consts[True], hgrn_params, o_f)

        gla_out = None
        for rev in (False, True):
            w2 = jnp.zeros((LANES, GW), F32)
            w2 = w2.at[rev * GLA_GATE_RANK:(rev + 1) * GLA_GATE_RANK].set(
                _pad_heads(gla_gk_w2[i, int(rev)], GLA_HEADS, GLA_HEAD_K))
            gb = _pad_heads(gla_gk_b[i, int(rev)][None, :], GLA_HEADS, GLA_HEAD_K)
            gla_params = (w2.astype(BF16), gb.astype(F32), jnp.tile(gla_norm[i], GLA_HEADS)[None, :].astype(F32))
            gla_out = _gated_scan(h, "gla", rev, scan_consts[rev], gla_params, gla_out)
        y_g = gla_out

        dt_bias = jnp.zeros((1, LANES), F32).at[0, :2 * SSD_HEADS].set(ssd_dt_bias[i].astype(F32).reshape(-1))
        a_coef = jnp.zeros((1, LANES), F32).at[0, :2 * SSD_HEADS].set(-jnp.exp(ssd_a_log[i].astype(F32)).reshape(-1))
        ssd_out = None
        for rev in (False, True):
            ssd_params = (ssd_conv_w[i].astype(F32), ssd_conv_b[i][None, :].astype(F32), dt_bias, a_coef,
                          _tri(ssd_chunk, rev), _ssd_expand(rev), jnp.repeat(ssd_d[i].astype(F32), SSD_HEAD_DIM)[None, :],
                          ssd_norm[i][None, :].astype(F32))
            ssd_out = _ssd(h, rev, ssd_params, ssd_out)
        y_s = ssd_out

        flat = lambda a: a.reshape(t, GW)
        xf, xb = _out_proj_ln(flat(y_a), flat(y_h), flat(y_s), flat(y_g), xf, w_out[i].astype(BF16),
                              ln_g[i][None, :].astype(F32), ln_b[i][None, :].astype(F32))
    return xf.reshape(bsz, seq, d)
```

```python
import functools

import numpy as np
import jax
import jax.numpy as jnp
from jax import lax
from jax.experimental import pallas as pl
from jax.experimental.pallas import tpu as pltpu

F32 = jnp.float32
BF16 = jnp.bfloat16

D_MODEL = 1024
DEPTH = 2
GROUP_WIDTH = 512
GRID_W = 64
ATTN_HEADS = 8
ATTN_KV_HEADS = 2
ATTN_HEAD_DIM = 64
ROPE_THETA = 10000.0
HGRN_HEADS = 4
HGRN_EXPAND = 128
SSD_HEADS = 8
SSD_HEAD_DIM = 64
SSD_GROUPS = 2
SSD_STATE = 128
SSD_CONV_WIDTH = 5
GLA_HEADS = 4
GLA_HEAD_K = 64
GLA_HEAD_V = 128
GLA_GATE_RANK = 16
GLA_GATE_NORMALIZER = 16.0
DEEPNORM_ALPHA = (2 * DEPTH) ** 0.25

LANES = 128
SUBLANES = 8
HALO = 16
GW = GROUP_WIDTH

(G_AQ, G_AZ, G_AKV, G_HQ, G_HFF, G_HFB, G_HI, G_HZ,
 G_SX, G_SBC, G_SZ, G_GQ, G_GK, G_GV, G_GZ) = range(15)
N_GROUPS = 15
NARROW_DT = G_AKV * 4 + 2
NARROW_LOW = G_AKV * 4 + 3
N_PROJ = N_GROUPS * GW

LIN_CHUNK = 64
LIN_SUB = 4
SSD_CHUNK = 128
SCAN_TOKENS = 1024
SCAN_PRE_ROWS = 128
ATTN_TQ = 256
ATTN_KEY_BLOCKS = 2
VMEM_LIMIT = 48 * 1024 * 1024


def _dot(a, b):
    return jnp.dot(a, b, preferred_element_type=F32)


def _dot_nt(a, b):
    return lax.dot_general(a, b, (((1,), (1,)), ((), ())), preferred_element_type=F32)


def _dot_tn(a, b):
    return lax.dot_general(a, b, (((0,), (0,)), ((), ())), preferred_element_type=F32)


def _split3(x):
    hi = x.astype(BF16)
    r1 = x - hi.astype(F32)
    mid = r1.astype(BF16)
    lo = (r1 - mid.astype(F32)).astype(BF16)
    return hi, mid, lo


def _sel_dot(sel, x):
    hi, mid, lo = _split3(x)
    return _dot(sel, hi) + _dot(sel, mid) + _dot(sel, lo)


def _log_sigmoid(x):
    return jnp.minimum(x, 0.0) - jnp.log1p(jnp.exp(-jnp.abs(x)))


def _silu(x):
    return x * jax.nn.sigmoid(x)


def _softplus(x):
    return jnp.maximum(x, 0.0) + jnp.log1p(jnp.exp(-jnp.abs(x)))


def _params(sem):
    return pltpu.CompilerParams(dimension_semantics=sem, vmem_limit_bytes=VMEM_LIMIT)


def _matmul_kernel(x_ref, w_ref, o_ref):
    o_ref[...] = _dot(x_ref[...].astype(BF16), w_ref[...]).astype(o_ref.dtype)


def _in_proj(xb, w):
    t, d = xb.shape
    n = w.shape[1]
    tm = min(2048, t)
    tn = GW
    return pl.pallas_call(
        _matmul_kernel,
        grid=(t // tm, n // tn),
        in_specs=[pl.BlockSpec((tm, d), lambda i, j: (i, 0)),
                  pl.BlockSpec((d, tn), lambda i, j: (0, j))],
        out_specs=pl.BlockSpec((tm, tn), lambda i, j: (i, j)),
        out_shape=jax.ShapeDtypeStruct((t, n), BF16),
        compiler_params=_params(("parallel", "parallel")),
        name="in_proj",
    )(xb, w)


def _head_rms(x, blocksum, gain, eps=1e-6):
    xx = x * x
    hi = xx.astype(BF16)
    lo = (xx - hi.astype(F32)).astype(BF16)
    ssq = _dot(hi, blocksum) + _dot(lo, blocksum)
    return x * lax.rsqrt(ssq * (1.0 / ATTN_HEAD_DIM) + eps) * gain


def _rope(x, cos, sin_signed):
    w = x.shape[1]
    lane = lax.broadcasted_iota(jnp.int32, x.shape, 1)
    first = (lane & 31) < 16
    partner = jnp.where(first, pltpu.roll(x, w - 16, 1), pltpu.roll(x, 16, 1))
    return x * cos + partner * sin_signed


def _attn_kernel(q_ref, z_ref, kv_ref, cos_ref, sin_ref, cosk_ref, sink_ref, qg_ref, kg_ref,
                 bs_ref, o_ref, kd_ref, vd_ref, *, seq, prep_rows):
    n = pl.program_id(1)

    @pl.when(n == 0)
    def _prep_kv():
        def body(i, carry):
            r0 = pl.multiple_of(i * prep_rows, prep_rows)
            kv = kv_ref[pl.ds(r0, prep_rows), :].astype(F32)
            k = kv[:, :LANES]
            v = kv[:, LANES:]
            k = _head_rms(k, bs_ref[0:LANES, 0:LANES], kg_ref[...])
            k = _rope(k, cosk_ref[pl.ds(r0, prep_rows), :], sink_ref[pl.ds(r0, prep_rows), :])
            lane = lax.broadcasted_iota(jnp.int32, k.shape, 1)
            low = lane < ATTN_HEAD_DIM
            k_sw = pltpu.roll(k, ATTN_HEAD_DIM, 1)
            v_sw = pltpu.roll(v, ATTN_HEAD_DIM, 1)
            kd_ref[0, pl.ds(r0, prep_rows), :] = jnp.where(low, k, k_sw).astype(BF16)
            kd_ref[1, pl.ds(r0, prep_rows), :] = jnp.where(low, k_sw, k).astype(BF16)
            vd_ref[0, pl.ds(r0, prep_rows), :] = jnp.where(low, v, v_sw).astype(BF16)
            vd_ref[1, pl.ds(r0, prep_rows), :] = jnp.where(low, v_sw, v).astype(BF16)
            return carry
        lax.fori_loop(0, seq // prep_rows, body, 0)

    q = _head_rms(q_ref[...].astype(F32), bs_ref[...], qg_ref[...])
    cos = jnp.concatenate([cos_ref[...]] * 4, axis=1)
    sin = jnp.concatenate([sin_ref[...]] * 4, axis=1)
    q = _rope(q, cos, sin) * (ATTN_HEAD_DIM ** -0.5)
    lane = lax.broadcasted_iota(jnp.int32, (q.shape[0], LANES), 1)
    low = lane < ATTN_HEAD_DIM
    group = ATTN_HEADS // ATTN_KV_HEADS
    kb_len = seq // ATTN_KEY_BLOCKS

    q_heads = []
    for hd in range(ATTN_HEADS):
        pair, e = divmod(hd, 2)
        qp = q[:, pair * LANES:(pair + 1) * LANES]
        q_heads.append(jnp.where(low if e == 0 else jnp.logical_not(low), qp, 0.0).astype(BF16))

    def scores(item):
        hd, kb = divmod(item, ATTN_KEY_BLOCKS)
        return _dot_nt(q_heads[hd], kd_ref[hd // group, kb * kb_len:(kb + 1) * kb_len, :])

    outs = []
    n_items = ATTN_HEADS * ATTN_KEY_BLOCKS
    s_next = scores(0)
    m_run = l_run = o_run = None
    for item in range(n_items):
        hd, kb = divmod(item, ATTN_KEY_BLOCKS)
        s = s_next
        if item + 1 < n_items:
            s_next = scores(item + 1)
        v_blk = vd_ref[hd // group, kb * kb_len:(kb + 1) * kb_len, :]
        m_blk = jnp.max(s, axis=-1, keepdims=True)
        if kb == 0:
            m_run = m_blk
            p = jnp.exp(s - m_run)
            l_run = jnp.sum(p, axis=-1, keepdims=True)
            o_run = _dot(p.astype(BF16), v_blk)
        else:
            m_new = jnp.maximum(m_run, m_blk)
            alpha = jnp.exp(m_run - m_new)
            p = jnp.exp(s - m_new)
            l_run = alpha * l_run + jnp.sum(p, axis=-1, keepdims=True)
            o_run = alpha * o_run + _dot(p.astype(BF16), v_blk)
            m_run = m_new
        if kb == ATTN_KEY_BLOCKS - 1:
            outs.append(o_run / l_run)
    for pair in range(ATTN_HEADS // 2):
        cols = slice(pair * LANES, (pair + 1) * LANES)
        o_pair = jnp.where(low, outs[2 * pair], outs[2 * pair + 1])
        o_ref[:, cols] = (o_pair * _silu(z_ref[:, cols].astype(F32))).astype(o_ref.dtype)


def _attention(h, rope, q_gain, k_gain, blocksum):
    bsz, seq, _ = h.shape
    tq = min(ATTN_TQ, seq)
    prep_rows = min(512, seq)
    cos2, sin2 = rope
    kernel = functools.partial(_attn_kernel, seq=seq, prep_rows=prep_rows)
    return pl.pallas_call(
        kernel,
        grid=(bsz, seq // tq),
        in_specs=[
            pl.BlockSpec((None, tq, GW), lambda b, n: (b, n, G_AQ)),
            pl.BlockSpec((None, tq, GW), lambda b, n: (b, n, G_AZ)),
            pl.BlockSpec((None, seq, 2 * LANES), lambda b, n: (b, 0, G_AKV * 2)),
            pl.BlockSpec((tq, LANES), lambda b, n: (n, 0)),
            pl.BlockSpec((tq, LANES), lambda b, n: (n, 0)),
            pl.BlockSpec((seq, LANES), lambda b, n: (0, 0)),
            pl.BlockSpec((seq, LANES), lambda b, n: (0, 0)),
            pl.BlockSpec((1, GW), lambda b, n: (0, 0)),
            pl.BlockSpec((1, LANES), lambda b, n: (0, 0)),
            pl.BlockSpec((GW, GW), lambda b, n: (0, 0)),
        ],
        out_specs=pl.BlockSpec((None, tq, GW), lambda b, n: (b, n, 0)),
        out_shape=jax.ShapeDtypeStruct((bsz, seq, GW), BF16),
        scratch_shapes=[pltpu.VMEM((2, seq, LANES), BF16), pltpu.VMEM((2, seq, LANES), BF16)],
        compiler_params=_params(("arbitrary", "arbitrary")),
        name="attention",
    )(h, h, h, cos2, sin2, cos2, sin2, q_gain, k_gain, blocksum)


def _scan_constants(chunk, tb, reverse):
    t = np.arange(chunk)[:, None]
    u = np.arange(chunk)[None, :]
    tri = (u >= t) if reverse else (u <= t)
    q_half = 0 if reverse else 1
    masks = []
    levels = []
    h = chunk // 2
    while h >= LIN_SUB:
        same_block = (t // (2 * h)) == (u // (2 * h))
        masks.append(same_block & ((t // h) % 2 == q_half) & ((u // h) % 2 == 1 - q_half))
        levels.append((h, q_half))
        h //= 2
    masks = np.stack(masks, axis=0).astype(np.float32)
    tri_blocks = np.kron(np.eye(tb // chunk), tri)
    return jnp.asarray(tri_blocks, BF16), jnp.asarray(masks, F32), tuple(levels)


def _scan_kernel(*refs, mode, reverse, final, chunk, tb, levels, pre_rows):
    it = iter(refs)
    if mode == "hgrn":
        q_ref, f_ref, v_ref = next(it), next(it), next(it)
        lb_ref = next(it)
    else:
        q_ref, k_ref, v_ref, low_ref = next(it), next(it), next(it), next(it)
        w2_ref, gb_ref = next(it), next(it)
    if final:
        z_ref, of_ref, gain_ref = next(it), next(it), next(it)
    stack_ref, lmask_ref = next(it), next(it)
    o_ref = next(it)
    st_ref, qs_ref, ks_ref, gs_ref, bs_ref = (next(it) for _ in range(5))
    if final:
        os_ref = next(it)

    n = pl.program_id(1)

    @pl.when(n == 0)
    def _reset():
        st_ref[...] = jnp.zeros_like(st_ref)

    if mode == "gla":
        gs_ref[...] = _dot(low_ref[...], w2_ref[...]) + gb_ref[...]
    for r in range(0, tb, pre_rows):
        rows = slice(r, r + pre_rows)
        if mode == "hgrn":
            qr = q_ref[rows, :].astype(F32)
            qs_ref[rows, :] = _silu(qr) * (HGRN_EXPAND ** -0.5)
            f = f_ref[rows, :].astype(F32)
            e = jnp.exp(-jnp.abs(f))
            r = 1.0 / (1.0 + e)
            pos = f >= 0.0
            sig = jnp.where(pos, r, e * r)
            sig_neg = jnp.where(pos, e * r, r)
            lb = lb_ref[...]
            om = 1.0 - lb
            log_sig = jnp.minimum(f, 0.0) + jnp.log(r)
            g_rows = jnp.where(lb > 0.0, jnp.log(lb + om * sig), jnp.log(om) + log_sig)
            ks_ref[rows, :] = om * sig_neg
        else:
            qs_ref[rows, :] = q_ref[rows, :].astype(F32) * (GLA_HEAD_K ** -0.5)
            ks_ref[rows, :] = k_ref[rows, :].astype(F32)
            g_rows = _log_sigmoid(gs_ref[rows, :]) * (1.0 / GLA_GATE_NORMALIZER)
        gs_ref[rows, :] = g_rows
        bs_ref[rows, :] = _sel_dot(stack_ref[...], g_rows)

    nc = tb // chunk
    shift = (chunk - 1) if reverse else 1
    row = lax.broadcasted_iota(jnp.int32, (chunk, LANES), 0)
    rr = lax.broadcasted_iota(jnp.int32, (chunk, chunk), 0)
    cc = lax.broadcasted_iota(jnp.int32, (chunk, chunk), 1)
    diff = (cc - rr) if reverse else (rr - cc)
    sub_edge = (LIN_SUB - 1) if reverse else 0
    keep = (row & (LIN_SUB - 1)) != sub_edge
    end_row = 0 if reverse else chunk - 1

    def body(ci, carry):
        c = (nc - 1 - ci) if reverse else ci
        r0 = pl.multiple_of(c * chunk, chunk)
        heads = range(4)
        lanes = [slice(hh * LANES, (hh + 1) * LANES) for hh in heads]
        q = [qs_ref[pl.ds(r0, chunk), ln] for ln in lanes]
        k = [ks_ref[pl.ds(r0, chunk), ln] for ln in lanes]
        g = [gs_ref[pl.ds(r0, chunk), ln] for ln in lanes]
        v = [v_ref[pl.ds(r0, chunk), ln] for ln in lanes]
        b = [bs_ref[pl.ds(r0, chunk), ln] for ln in lanes]

        o = []
        for hh in heads:
            st = st_ref[hh]
            total = b[hh][end_row:end_row + 1, :]
            o.append(_dot_nt((q[hh] * jnp.exp(b[hh])).astype(BF16), st.astype(BF16)))
            k_end = (k[hh] * jnp.exp(total - b[hh])).astype(BF16)
            st_ref[hh] = jnp.exp(total) * st + _dot_tn(v[hh], k_end)

        scores = [jnp.zeros((chunk, chunk), F32) for _ in heads]
        for li, (h, q_half) in enumerate(levels):
            is_q = ((row // h) & 1) == q_half
            ref_row = h - 1 if reverse else h
            for hh in heads:
                b3 = b[hh].reshape(chunk // (2 * h), 2 * h, LANES)
                d3 = b3 - b3[:, ref_row:ref_row + 1, :]
                log_e = -jnp.abs(d3.reshape(chunk, LANES))
                x = (jnp.where(is_q, q[hh], k[hh]) * jnp.exp(log_e)).astype(BF16)
                scores[hh] = scores[hh] + _dot_nt(x, x) * lmask_ref[li]

        for hh in heads:
            f0 = jnp.where(keep, jnp.exp(g[hh]), 0.0)
            a = k[hh]
            band = jnp.where(diff == 0, jnp.sum(q[hh] * a, axis=-1, keepdims=True), 0.0)
            for d in range(1, LIN_SUB):
                a = f0 * pltpu.roll(a, shift, 0)
                band = jnp.where(diff == d, jnp.sum(q[hh] * a, axis=-1, keepdims=True), band)
            scores[hh] = scores[hh] + band

        for hh in heads:
            out = o[hh] + _dot(scores[hh].astype(BF16), v[hh])
            if final:
                os_ref[pl.ds(r0, chunk), lanes[hh]] = out
            else:
                o_ref[pl.ds(r0, chunk), lanes[hh]] = out
        return carry

    lax.fori_loop(0, nc, body, 0, unroll=4)

    if final:
        o = os_ref[...] + of_ref[...]
        if mode == "hgrn":
            ms = jnp.mean(o * o, axis=-1, keepdims=True)
            y = o * lax.rsqrt(ms + 1e-6) * gain_ref[...]
        else:
            parts = []
            for hh in range(4):
                oh = o[:, hh * LANES:(hh + 1) * LANES]
                ms = jnp.mean(oh * oh, axis=-1, keepdims=True)
                parts.append(oh * lax.rsqrt(ms + 1e-6))
            y = jnp.concatenate(parts, axis=1) * gain_ref[...]
        o_ref[...] = (y * _silu(z_ref[...].astype(F32))).astype(o_ref.dtype)


def _gated_scan(h, mode, reverse, consts, params, o_fwd=None):
    bsz, seq, _ = h.shape
    tb = min(SCAN_TOKENS, seq)
    nb = seq // tb
    chunk = LIN_CHUNK
    stack, lmask, levels = consts
    final = reverse

    def blk(n):
        return (nb - 1 - n) if reverse else n

    def col(group):
        return pl.BlockSpec((None, tb, GW), lambda b, n: (b, blk(n), group))

    def full(arr):
        nd = arr.ndim
        return pl.BlockSpec(arr.shape, lambda b, n: (0,) * nd)

    if mode == "hgrn":
        lower_bound, gain = params
        ins = [h, h, h, lower_bound]
        specs = [col(G_HQ), col(G_HFB if reverse else G_HFF), col(G_HI), full(lower_bound)]
        z_group = G_HZ
    else:
        w2, gb, gain = params
        ins = [h, h, h, h, w2, gb]
        specs = [col(G_GQ), col(G_GK), col(G_GV),
                 pl.BlockSpec((None, tb, LANES), lambda b, n: (b, blk(n), NARROW_LOW)),
                 full(w2), full(gb)]
        z_group = G_GZ
    if final:
        ins += [h, o_fwd, gain]
        specs += [col(z_group), pl.BlockSpec((None, tb, GW), lambda b, n: (b, blk(n), 0)), full(gain)]
    ins += [stack, lmask]
    specs += [full(stack), full(lmask)]
    scratch = [pltpu.VMEM((4, LANES, LANES), F32)] + [pltpu.VMEM((tb, GW), F32)] * 4
    if final:
        scratch.append(pltpu.VMEM((tb, GW), F32))
    kernel = functools.partial(_scan_kernel, mode=mode, reverse=reverse, final=final,
                               chunk=chunk, tb=tb, levels=levels, pre_rows=stack.shape[0])
    return pl.pallas_call(
        kernel,
        grid=(bsz, nb),
        in_specs=specs,
        out_specs=pl.BlockSpec((None, tb, GW), lambda b, n: (b, blk(n), 0)),
        out_shape=jax.ShapeDtypeStruct((bsz, seq, GW), BF16 if final else F32),
        scratch_shapes=scratch,
        compiler_params=_params(("arbitrary", "arbitrary")),
        name=f"{mode}_{'bwd' if reverse else 'fwd'}",
    )(*ins)


def _ssd_kernel(*refs, reverse, chunk, tb, nb):
    it = iter(refs)
    if not reverse:
        x_ref, xp_ref, xn_ref, bc_ref, bcp_ref, bcn_ref, cw_ref, cb_ref = (next(it) for _ in range(8))
    else:
        ux_ref, ubc_ref = next(it), next(it)
    dt_ref, dtb_ref, acoef_ref, tri_ref, expand_ref = (next(it) for _ in range(5))
    if reverse:
        z_ref, yf_ref, dskip_ref, gain_ref = (next(it) for _ in range(4))
        o_ref = next(it)
    else:
        o_ref, ux_ref, ubc_ref = next(it), next(it), next(it)
    st_ref, dts_ref, as_ref = next(it), next(it), next(it)
    if reverse:
        ys_ref = next(it)
    else:
        ext_ref = next(it)

    n = pl.program_id(1)
    pos = (nb - 1 - n) if reverse else n

    @pl.when(n == 0)
    def _reset():
        st_ref[...] = jnp.zeros_like(st_ref)

    if not reverse:
        halves = ((x_ref, xp_ref, xn_ref, ux_ref), (bc_ref, bcp_ref, bcn_ref, ubc_ref))
        for hi, (cur, prev, nxt, dst) in enumerate(halves):
            cols = slice(hi * GW, (hi + 1) * GW)
            ext_ref[0:HALO, :] = jnp.where(pos > 0, prev[...].astype(F32), 0.0)
            ext_ref[HALO:HALO + tb, :] = cur[...].astype(F32)
            ext_ref[HALO + tb:2 * HALO + tb, :] = jnp.where(pos < nb - 1, nxt[...].astype(F32), 0.0)
            acc = jnp.broadcast_to(cb_ref[:, cols], (tb, GW))
            for tap in range(SSD_CONV_WIDTH):
                acc = acc + cw_ref[tap:tap + 1, cols] * ext_ref[HALO - 2 + tap:HALO - 2 + tap + tb, :]
            dst[...] = _silu(acc).astype(dst.dtype)

    dt = _softplus(dt_ref[...].astype(F32) + dtb_ref[...])
    dts_ref[...] = dt
    as_ref[...] = dt * acoef_ref[...]

    nc = tb // chunk
    head0 = SSD_HEADS if reverse else 0
    rr = lax.broadcasted_iota(jnp.int32, (chunk, chunk), 0)
    cc = lax.broadcasted_iota(jnp.int32, (chunk, chunk), 1)
    valid = (rr <= cc) if reverse else (rr >= cc)
    lane = lax.broadcasted_iota(jnp.int32, (chunk, LANES), 1)
    low = lane < SSD_HEAD_DIM
    end_row = 0 if reverse else chunk - 1

    def body(ci, carry):
        c = (nc - 1 - ci) if reverse else ci
        r0 = pl.multiple_of(c * chunk, chunk)
        rows = pl.ds(r0, chunk)
        dtc = dts_ref[rows, :]
        acs = _sel_dot(tri_ref[...], as_ref[rows, :])
        a_end = acs[end_row:end_row + 1, :]
        acs_t = jnp.transpose(acs)
        narrow = jnp.concatenate([dtc, jnp.exp(acs), jnp.exp(a_end - acs)], axis=0)
        hi = narrow.astype(BF16)
        lo = (narrow - hi.astype(F32)).astype(BF16)
        wide = _dot(hi, expand_ref[...]) + _dot(lo, expand_ref[...])
        dt_x = wide[0:chunk]
        e_acs = wide[chunk:2 * chunk]
        e_rest = wide[2 * chunk:3 * chunk]

        shared = []
        for g in range(SSD_GROUPS):
            bg = ubc_ref[rows, g * LANES:(g + 1) * LANES]
            cg = ubc_ref[rows, (SSD_GROUPS + g) * LANES:(SSD_GROUPS + g + 1) * LANES]
            st = st_ref[g]
            shared.append((bg, st, _dot_nt(cg, bg), _dot(cg, st.astype(BF16))))
        for g in range(SSD_GROUPS):
            bg, st, gm, c_st = shared[g]
            wx, e_end = [], []
            for jp in range(2):
                pair = 2 * g + jp
                pcols = slice(pair * LANES, (pair + 1) * LANES)
                ea = e_acs[:, pcols]
                xdt = ux_ref[rows, pcols] * dt_x[:, pcols]
                wx.append((e_rest[:, pcols] * xdt).astype(BF16))
                e_end.append(ea[end_row:end_row + 1, :])
                acc = c_st[:, jp * LANES:(jp + 1) * LANES] * ea
                for e in range(2):
                    hl = head0 + 2 * pair + e
                    dmat = acs[:, hl:hl + 1] - acs_t[hl:hl + 1, :]
                    lm = jnp.where(valid, jnp.exp(jnp.minimum(dmat, 0.0)), 0.0)
                    m = (gm * lm).astype(BF16)
                    xm = jnp.where(low if e == 0 else jnp.logical_not(low), xdt, 0.0).astype(BF16)
                    acc = acc + _dot(m, xm)
                if reverse:
                    ys_ref[rows, pcols] = acc
                else:
                    o_ref[rows, pcols] = acc
            st_ref[g] = (jnp.concatenate(e_end, axis=1) * st
                         + _dot_tn(bg, jnp.concatenate(wx, axis=1)))
        return carry

    lax.fori_loop(0, nc, body, 0, unroll=2)

    if reverse:
        y = ys_ref[...] + yf_ref[...] + dskip_ref[...] * ux_ref[...]
        y = y * _silu(z_ref[...].astype(F32))
        ms = jnp.mean(y * y, axis=-1, keepdims=True)
        o_ref[...] = (y * lax.rsqrt(ms + 1e-6) * gain_ref[...]).astype(o_ref.dtype)


def _ssd(h, reverse, params, fwd=None):
    bsz, seq, _ = h.shape
    tb = min(SCAN_TOKENS, seq)
    nb = seq // tb
    chunk = min(SSD_CHUNK, tb)
    conv_w, conv_b, dt_bias, a_coef, tri, expand, d_skip, gain = params
    hb = tb // HALO
    nhb = seq // HALO

    def blk(n):
        return (nb - 1 - n) if reverse else n

    def col(group, width=GW, scale=1):
        return pl.BlockSpec((None, tb, width), lambda b, n: (b, blk(n), group * scale))

    def prev(group):
        return pl.BlockSpec((None, HALO, GW), lambda b, n: (b, jnp.maximum(blk(n) * hb - 1, 0), group))

    def nxt(group):
        return pl.BlockSpec((None, HALO, GW), lambda b, n: (b, jnp.minimum((blk(n) + 1) * hb, nhb - 1), group))

    def full(arr):
        nd = arr.ndim
        return pl.BlockSpec(arr.shape, lambda b, n: (0,) * nd)

    plain = pl.BlockSpec((None, tb, GW), lambda b, n: (b, blk(n), 0))
    dt_spec = pl.BlockSpec((None, tb, LANES), lambda b, n: (b, blk(n), NARROW_DT))
    state = pltpu.VMEM((SSD_GROUPS, SSD_STATE, 2 * LANES), F32)
    narrow = pltpu.VMEM((tb, LANES), F32)
    kernel = functools.partial(_ssd_kernel, reverse=reverse, chunk=chunk, tb=tb, nb=nb)
    if not reverse:
        return pl.pallas_call(
            kernel,
            grid=(bsz, nb),
            in_specs=[col(G_SX), prev(G_SX), nxt(G_SX), col(G_SBC), prev(G_SBC), nxt(G_SBC),
                      full(conv_w), full(conv_b), dt_spec, full(dt_bias), full(a_coef), full(tri),
                      full(expand)],
            out_specs=[plain, plain, plain],
            out_shape=[jax.ShapeDtypeStruct((bsz, seq, GW), F32),
                       jax.ShapeDtypeStruct((bsz, seq, GW), F32),
                       jax.ShapeDtypeStruct((bsz, seq, GW), BF16)],
            scratch_shapes=[state, narrow, narrow, pltpu.VMEM((tb + 2 * HALO, GW), F32)],
            compiler_params=_params(("arbitrary", "arbitrary")),
            name="ssd_fwd",
        )(h, h, h, h, h, h, conv_w, conv_b, h, dt_bias, a_coef, tri, expand)
    y_fwd, u_x, u_bc = fwd
    return pl.pallas_call(
        kernel,
        grid=(bsz, nb),
        in_specs=[plain, plain, dt_spec, full(dt_bias), full(a_coef), full(tri), full(expand),
                  col(G_SZ), plain, full(d_skip), full(gain)],
        out_specs=plain,
        out_shape=jax.ShapeDtypeStruct((bsz, seq, GW), BF16),
        scratch_shapes=[state, narrow, narrow, pltpu.VMEM((tb, GW), F32)],
        compiler_params=_params(("arbitrary", "arbitrary")),
        name="ssd_bwd",
    )(u_x, u_bc, h, dt_bias, a_coef, tri, expand, h, y_fwd, d_skip, gain)


def _out_kernel(ya_ref, yh_ref, ys_ref, yg_ref, x_ref, w_ref, g_ref, b_ref, xo_ref, xb_ref):
    acc = _dot(ya_ref[...], w_ref[0:GW, :])
    acc = acc + _dot(yh_ref[...], w_ref[GW:2 * GW, :])
    acc = acc + _dot(ys_ref[...], w_ref[2 * GW:3 * GW, :])
    acc = acc + _dot(yg_ref[...], w_ref[3 * GW:4 * GW, :])
    r = DEEPNORM_ALPHA * x_ref[...] + acc
    mu = jnp.mean(r, axis=-1, keepdims=True)
    rc = r - mu
    var = jnp.mean(rc * rc, axis=-1, keepdims=True)
    y = rc * lax.rsqrt(var + 1e-5) * g_ref[...] + b_ref[...]
    xo_ref[...] = y
    xb_ref[...] = y.astype(BF16)


def _out_proj_ln(ya, yh, ys, yg, x, w, ln_g, ln_b):
    t, d = x.shape
    tm = min(512, t)
    row = lambda width: pl.BlockSpec((tm, width), lambda i: (i, 0))
    const = lambda arr: pl.BlockSpec(arr.shape, lambda i: (0, 0))
    return pl.pallas_call(
        _out_kernel,
        grid=(t // tm,),
        in_specs=[row(GW), row(GW), row(GW), row(GW), row(d), const(w), const(ln_g), const(ln_b)],
        out_specs=[row(d), row(d)],
        out_shape=[jax.ShapeDtypeStruct((t, d), F32), jax.ShapeDtypeStruct((t, d), BF16)],
        compiler_params=_params(("parallel",)),
        name="out_proj_ln",
    )(ya, yh, ys, yg, x, w, ln_g, ln_b)


def _pad_heads(w, heads, width):
    d = w.shape[0]
    w = w.reshape(d, heads, width)
    w = jnp.pad(w, ((0, 0), (0, 0), (0, LANES - width)))
    return w.reshape(d, heads * LANES)


def _layout_w_in(w):
    d = w.shape[0]
    sizes = (512, 128, 128, 512, 512, 512, 512, 512, 512, 1024, 8, 8, 512, 256, 256, 512, 16, 16, 512)
    offs = np.concatenate([[0], np.cumsum(sizes)])
    (a_q, a_k, a_v, a_z, h_q, h_ff, h_fb, h_i, h_z,
     s_xbc, s_dtf, s_dtb, s_z, g_q, g_k, g_v, g_lf, g_lb, g_z) = [
        w[:, int(offs[i]):int(offs[i + 1])] for i in range(len(sizes))]
    zeros = lambda n: jnp.zeros((d, n), w.dtype)
    groups = [
        a_q, a_z, jnp.concatenate([a_k, a_v, s_dtf, s_dtb, zeros(112), g_lf, g_lb, zeros(96)], axis=1),
        h_q, h_ff, h_fb, h_i, h_z,
        s_xbc[:, :512], s_xbc[:, 512:], s_z,
        _pad_heads(g_q, GLA_HEADS, GLA_HEAD_K), _pad_heads(g_k, GLA_HEADS, GLA_HEAD_K), g_v, g_z,
    ]
    return jnp.concatenate(groups, axis=1).astype(BF16)


def _rope_tables(seq):
    rows = seq // GRID_W
    row_pos = jnp.repeat(jnp.arange(rows, dtype=F32), GRID_W)
    col_pos = jnp.tile(jnp.arange(GRID_W, dtype=F32), rows)
    axis_dim = ATTN_HEAD_DIM // 2
    inv_freq = jnp.power(ROPE_THETA, -jnp.arange(0, axis_dim, 2, dtype=F32) / axis_dim)
    ang_r = row_pos[:, None] * inv_freq
    ang_c = col_pos[:, None] * inv_freq
    cos = jnp.concatenate([jnp.cos(ang_r)] * 2 + [jnp.cos(ang_c)] * 2, axis=1)
    sin = jnp.concatenate([-jnp.sin(ang_r), jnp.sin(ang_r), -jnp.sin(ang_c), jnp.sin(ang_c)], axis=1)
    return jnp.tile(cos, (1, 2)), jnp.tile(sin, (1, 2))


def _ssd_expand(reverse):
    m = np.zeros((LANES, GW), np.float32)
    for hd in range(SSD_HEADS):
        m[hd + (SSD_HEADS if reverse else 0), hd * SSD_HEAD_DIM:(hd + 1) * SSD_HEAD_DIM] = 1.0
    return jnp.asarray(m, BF16)


def _tri(chunk, reverse):
    t = np.arange(chunk)[:, None]
    u = np.arange(chunk)[None, :]
    return jnp.asarray((u >= t) if reverse else (u <= t), BF16)


def kernel(x, w_in, attn_q_norm, attn_k_norm, hgrn_lb_logits, hgrn_norm, ssd_conv_w, ssd_conv_b,
           ssd_dt_bias, ssd_a_log, ssd_d, ssd_norm, gla_gk_w2, gla_gk_b, gla_norm, w_out, ln_g, ln_b):
    bsz, seq, d = x.shape
    t = bsz * seq
    rope = _rope_tables(seq)
    blocksum = jnp.asarray(np.kron(np.eye(GW // ATTN_HEAD_DIM), np.ones((ATTN_HEAD_DIM, ATTN_HEAD_DIM))), BF16)
    lower_bounds = jnp.cumsum(jax.nn.softmax(hgrn_lb_logits.astype(F32), axis=0), axis=0)
    lower_bounds = lower_bounds - lower_bounds[0]
    scan_consts = {rev: _scan_constants(LIN_CHUNK, min(SCAN_PRE_ROWS, seq), rev) for rev in (False, True)}
    ssd_chunk = min(SSD_CHUNK, min(SCAN_TOKENS, seq))

    xf = x.reshape(t, d)
    xb = xf
    for i in range(DEPTH):
        h = _in_proj(xb, _layout_w_in(w_in[i])).reshape(bsz, seq, N_PROJ)

        y_a = _attention(h, rope, jnp.tile(attn_q_norm[i], ATTN_HEADS)[None, :].astype(F32),
                         jnp.tile(attn_k_norm[i], 2)[None, :].astype(F32), blocksum)

        lb = jnp.maximum(lower_bounds[i], 0.0)[None, :]
        hgrn_params = (lb, hgrn_norm[i][None, :].astype(F32))
        o_f = _gated_scan(h, "hgrn", False, scan_consts[False], hgrn_params)
        y_h = _gated_scan(h, "hgrn", True, scan_consts[True], hgrn_params, o_f)

        gla_out = None
        for rev in (False, True):
            w2 = jnp.zeros((LANES, GW), F32)
            w2 = w2.at[rev * GLA_GATE_RANK:(rev + 1) * GLA_GATE_RANK].set(
                _pad_heads(gla_gk_w2[i, int(rev)], GLA_HEADS, GLA_HEAD_K))
            gb = _pad_heads(gla_gk_b[i, int(rev)][None, :], GLA_HEADS, GLA_HEAD_K)
            gla_params = (w2.astype(BF16), gb.astype(F32), jnp.tile(gla_norm[i], GLA_HEADS)[None, :].astype(F32))
            gla_out = _gated_scan(h, "gla", rev, scan_consts[rev], gla_params, gla_out)
        y_g = gla_out

        dt_bias = jnp.zeros((1, LANES), F32).at[0, :2 * SSD_HEADS].set(ssd_dt_bias[i].astype(F32).reshape(-1))
        a_coef = jnp.zeros((1, LANES), F32).at[0, :2 * SSD_HEADS].set(-jnp.exp(ssd_a_log[i].astype(F32)).reshape(-1))
        ssd_out = None
        for rev in (False, True):
            ssd_params = (ssd_conv_w[i].astype(F32), ssd_conv_b[i][None, :].astype(F32), dt_bias, a_coef,
                          _tri(ssd_chunk, rev), _ssd_expand(rev), jnp.repeat(ssd_d[i].astype(F32), SSD_HEAD_DIM)[None, :],
                          ssd_norm[i][None, :].astype(F32))
            ssd_out = _ssd(h, rev, ssd_params, ssd_out)
        y_s = ssd_out

        flat = lambda a: a.reshape(t, GW)
        xf, xb = _out_proj_ln(flat(y_a), flat(y_h), flat(y_s), flat(y_g), xf, w_out[i].astype(BF16),
                              ln_g[i][None, :].astype(F32), ln_b[i][None, :].astype(F32))
    return xf.reshape(bsz, seq, d)
```

```python
import functools

import numpy as np
import jax
import jax.numpy as jnp
from jax import lax
from jax.experimental import pallas as pl
from jax.experimental.pallas import tpu as pltpu

F32 = jnp.float32
BF16 = jnp.bfloat16

D_MODEL = 1024
DEPTH = 2
GROUP_WIDTH = 512
GRID_W = 64
ATTN_HEADS = 8
ATTN_KV_HEADS = 2
ATTN_HEAD_DIM = 64
ROPE_THETA = 10000.0
HGRN_HEADS = 4
HGRN_EXPAND = 128
SSD_HEADS = 8
SSD_HEAD_DIM = 64
SSD_GROUPS = 2
SSD_STATE = 128
SSD_CONV_WIDTH = 5
GLA_HEADS = 4
GLA_HEAD_K = 64
GLA_HEAD_V = 128
GLA_GATE_RANK = 16
GLA_GATE_NORMALIZER = 16.0
DEEPNORM_ALPHA = (2 * DEPTH) ** 0.25

LANES = 128
SUBLANES = 8
HALO = 16
GW = GROUP_WIDTH

(G_AQ, G_AZ, G_AKV, G_HQ, G_HFF, G_HFB, G_HI, G_HZ,
 G_SX, G_SBC, G_SZ, G_GQ, G_GK, G_GV, G_GZ) = range(15)
N_GROUPS = 15
NARROW_DT = G_AKV * 4 + 2
NARROW_LOW = G_AKV * 4 + 3
N_PROJ = N_GROUPS * GW

LIN_CHUNK = 64
LIN_SUB = 4
SSD_CHUNK = 128
SCAN_TOKENS = 1024
SCAN_PRE_ROWS = 128
ATTN_TQ = 256
ATTN_KEY_BLOCKS = 2
VMEM_LIMIT = 48 * 1024 * 1024


def _dot(a, b):
    return jnp.dot(a, b, preferred_element_type=F32)


def _dot_nt(a, b):
    return lax.dot_general(a, b, (((1,), (1,)), ((), ())), preferred_element_type=F32)


def _dot_tn(a, b):
    return lax.dot_general(a, b, (((0,), (0,)), ((), ())), preferred_element_type=F32)


def _split3(x):
    hi = x.astype(BF16)
    r1 = x - hi.astype(F32)
    mid = r1.astype(BF16)
    lo = (r1 - mid.astype(F32)).astype(BF16)
    return hi, mid, lo


def _sel_dot(sel, x):
    hi, mid, lo = _split3(x)
    return _dot(sel, hi) + _dot(sel, mid) + _dot(sel, lo)


def _log_sigmoid(x):
    return jnp.minimum(x, 0.0) - jnp.log1p(jnp.exp(-jnp.abs(x)))


def _silu(x):
    return x * jax.nn.sigmoid(x)


def _softplus(x):
    return jnp.maximum(x, 0.0) + jnp.log1p(jnp.exp(-jnp.abs(x)))


def _params(sem):
    return pltpu.CompilerParams(dimension_semantics=sem, vmem_limit_bytes=VMEM_LIMIT)


def _matmul_kernel(x_ref, w_ref, o_ref):
    o_ref[...] = _dot(x_ref[...].astype(BF16), w_ref[...]).astype(o_ref.dtype)


def _in_proj(xb, w):
    t, d = xb.shape
    n = w.shape[1]
    tm = min(2048, t)
    tn = GW
    return pl.pallas_call(
        _matmul_kernel,
        grid=(t // tm, n // tn),
        in_specs=[pl.BlockSpec((tm, d), lambda i, j: (i, 0)),
                  pl.BlockSpec((d, tn), lambda i, j: (0, j))],
        out_specs=pl.BlockSpec((tm, tn), lambda i, j: (i, j)),
        out_shape=jax.ShapeDtypeStruct((t, n), BF16),
        compiler_params=_params(("parallel", "parallel")),
        name="in_proj",
    )(xb, w)


def _head_rms(x, blocksum, gain, eps=1e-6):
    xx = x * x
    hi = xx.astype(BF16)
    lo = (xx - hi.astype(F32)).astype(BF16)
    ssq = _dot(hi, blocksum) + _dot(lo, blocksum)
    return x * lax.rsqrt(ssq * (1.0 / ATTN_HEAD_DIM) + eps) * gain


def _rope(x, cos, sin_signed):
    w = x.shape[1]
    lane = lax.broadcasted_iota(jnp.int32, x.shape, 1)
    first = (lane & 31) < 16
    partner = jnp.where(first, pltpu.roll(x, w - 16, 1), pltpu.roll(x, 16, 1))
    return x * cos + partner * sin_signed


def _attn_kernel(q_ref, z_ref, kv_ref, cos_ref, sin_ref, cosk_ref, sink_ref, qg_ref, kg_ref,
                 bs_ref, o_ref, kd_ref, vd_ref, *, seq, prep_rows):
    n = pl.program_id(1)

    @pl.when(n == 0)
    def _prep_kv():
        def body(i, carry):
            r0 = pl.multiple_of(i * prep_rows, prep_rows)
            kv = kv_ref[pl.ds(r0, prep_rows), :].astype(F32)
            k = kv[:, :LANES]
            v = kv[:, LANES:]
            k = _head_rms(k, bs_ref[0:LANES, 0:LANES], kg_ref[...])
            k = _rope(k, cosk_ref[pl.ds(r0, prep_rows), :], sink_ref[pl.ds(r0, prep_rows), :])
            lane = lax.broadcasted_iota(jnp.int32, k.shape, 1)
            low = lane < ATTN_HEAD_DIM
            k_sw = pltpu.roll(k, ATTN_HEAD_DIM, 1)
            v_sw = pltpu.roll(v, ATTN_HEAD_DIM, 1)
            kd_ref[0, pl.ds(r0, prep_rows), :] = jnp.where(low, k, k_sw).astype(BF16)
            kd_ref[1, pl.ds(r0, prep_rows), :] = jnp.where(low, k_sw, k).astype(BF16)
            vd_ref[0, pl.ds(r0, prep_rows), :] = jnp.where(low, v, v_sw).astype(BF16)
            vd_ref[1, pl.ds(r0, prep_rows), :] = jnp.where(low, v_sw, v).astype(BF16)
            return carry
        lax.fori_loop(0, seq // prep_rows, body, 0)

    q = _head_rms(q_ref[...].astype(F32), bs_ref[...], qg_ref[...])
    cos = jnp.concatenate([cos_ref[...]] * 4, axis=1)
    sin = jnp.concatenate([sin_ref[...]] * 4, axis=1)
    q = _rope(q, cos, sin) * (ATTN_HEAD_DIM ** -0.5)
    lane = lax.broadcasted_iota(jnp.int32, (q.shape[0], LANES), 1)
    low = lane < ATTN_HEAD_DIM
    group = ATTN_HEADS // ATTN_KV_HEADS
    kb_len = seq // ATTN_KEY_BLOCKS

    q_heads = []
    for hd in range(ATTN_HEADS):
        pair, e = divmod(hd, 2)
        qp = q[:, pair * LANES:(pair + 1) * LANES]
        q_heads.append(jnp.where(low if e == 0 else jnp.logical_not(low), qp, 0.0).astype(BF16))

    def scores(item):
        hd, kb = divmod(item, ATTN_KEY_BLOCKS)
        return _dot_nt(q_heads[hd], kd_ref[hd // group, kb * kb_len:(kb + 1) * kb_len, :])

    outs = []
    n_items = ATTN_HEADS * ATTN_KEY_BLOCKS
    s_next = scores(0)
    m_run = l_run = o_run = None
    for item in range(n_items):
        hd, kb = divmod(item, ATTN_KEY_BLOCKS)
        s = s_next
        if item + 1 < n_items:
            s_next = scores(item + 1)
        v_blk = vd_ref[hd // group, kb * kb_len:(kb + 1) * kb_len, :]
        m_blk = jnp.max(s, axis=-1, keepdims=True)
        if kb == 0:
            m_run = m_blk
            p = jnp.exp(s - m_run)
            l_run = jnp.sum(p, axis=-1, keepdims=True)
            o_run = _dot(p.astype(BF16), v_blk)
        else:
            m_new = jnp.maximum(m_run, m_blk)
            alpha = jnp.exp(m_run - m_new)
            p = jnp.exp(s - m_new)
            l_run = alpha * l_run + jnp.sum(p, axis=-1, keepdims=True)
            o_run = alpha * o_run + _dot(p.astype(BF16), v_blk)
            m_run = m_new
        if kb == ATTN_KEY_BLOCKS - 1:
            outs.append(o_run / l_run)
    for pair in range(ATTN_HEADS // 2):
        cols = slice(pair * LANES, (pair + 1) * LANES)
        o_pair = jnp.where(low, outs[2 * pair], outs[2 * pair + 1])
        o_ref[:, cols] = (o_pair * _silu(z_ref[:, cols].astype(F32))).astype(o_ref.dtype)


def _attention(h, rope, q_gain, k_gain, blocksum):
    bsz, seq, _ = h.shape
    tq = min(ATTN_TQ, seq)
    prep_rows = min(512, seq)
    cos2, sin2 = rope
    kernel = functools.partial(_attn_kernel, seq=seq, prep_rows=prep_rows)
    return pl.pallas_call(
        kernel,
        grid=(bsz, seq // tq),
        in_specs=[
            pl.BlockSpec((None, tq, GW), lambda b, n: (b, n, G_AQ)),
            pl.BlockSpec((None, tq, GW), lambda b, n: (b, n, G_AZ)),
            pl.BlockSpec((None, seq, 2 * LANES), lambda b, n: (b, 0, G_AKV * 2)),
            pl.BlockSpec((tq, LANES), lambda b, n: (n, 0)),
            pl.BlockSpec((tq, LANES), lambda b, n: (n, 0)),
            pl.BlockSpec((seq, LANES), lambda b, n: (0, 0)),
            pl.BlockSpec((seq, LANES), lambda b, n: (0, 0)),
            pl.BlockSpec((1, GW), lambda b, n: (0, 0)),
            pl.BlockSpec((1, LANES), lambda b, n: (0, 0)),
            pl.BlockSpec((GW, GW), lambda b, n: (0, 0)),
        ],
        out_specs=pl.BlockSpec((None, tq, GW), lambda b, n: (b, n, 0)),
        out_shape=jax.ShapeDtypeStruct((bsz, seq, GW), BF16),
        scratch_shapes=[pltpu.VMEM((2, seq, LANES), BF16), pltpu.VMEM((2, seq, LANES), BF16)],
        compiler_params=_params(("arbitrary", "arbitrary")),
        name="attention",
    )(h, h, h, cos2, sin2, cos2, sin2, q_gain, k_gain, blocksum)


def _scan_constants(chunk, tb, reverse):
    t = np.arange(chunk)[:, None]
    u = np.arange(chunk)[None, :]
    tri = (u >= t) if reverse else (u <= t)
    q_half = 0 if reverse else 1
    masks = []
    levels = []
    h = chunk // 2
    while h >= LIN_SUB:
        same_block = (t // (2 * h)) == (u // (2 * h))
        masks.append(same_block & ((t // h) % 2 == q_half) & ((u // h) % 2 == 1 - q_half))
        levels.append((h, q_half))
        h //= 2
    masks = np.stack(masks, axis=0).astype(np.float32)
    tri_blocks = np.kron(np.eye(tb // chunk), tri)
    return jnp.asarray(tri_blocks, BF16), jnp.asarray(masks, F32), tuple(levels)


def _scan_kernel(*refs, mode, reverse, final, chunk, tb, levels, pre_rows):
    it = iter(refs)
    if mode == "hgrn":
        q_ref, f_ref, v_ref = next(it), next(it), next(it)
        lb_ref = next(it)
    else:
        q_ref, k_ref, v_ref, low_ref = next(it), next(it), next(it), next(it)
        w2_ref, gb_ref = next(it), next(it)
    if final:
        z_ref, of_ref, gain_ref = next(it), next(it), next(it)
    stack_ref, lmask_ref = next(it), next(it)
    o_ref = next(it)
    st_ref, qs_ref, ks_ref, gs_ref, bs_ref = (next(it) for _ in range(5))
    if final:
        os_ref = next(it)

    n = pl.program_id(1)

    @pl.when(n == 0)
    def _reset():
        st_ref[...] = jnp.zeros_like(st_ref)

    if mode == "gla":
        gs_ref[...] = _dot(low_ref[...], w2_ref[...]) + gb_ref[...]
    for r in range(0, tb, pre_rows):
        rows = slice(r, r + pre_rows)
        if mode == "hgrn":
            qr = q_ref[rows, :].astype(F32)
            qs_ref[rows, :] = _silu(qr) * (HGRN_EXPAND ** -0.5)
            f = f_ref[rows, :].astype(F32)
            e = jnp.exp(-jnp.abs(f))
            r = 1.0 / (1.0 + e)
            pos = f >= 0.0
            sig = jnp.where(pos, r, e * r)
            sig_neg = jnp.where(pos, e * r, r)
            lb = lb_ref[...]
            om = 1.0 - lb
            log_sig = jnp.minimum(f, 0.0) + jnp.log(r)
            g_rows = jnp.where(lb > 0.0, jnp.log(lb + om * sig), jnp.log(om) + log_sig)
            ks_ref[rows, :] = om * sig_neg
        else:
            qs_ref[rows, :] = q_ref[rows, :].astype(F32) * (GLA_HEAD_K ** -0.5)
            ks_ref[rows, :] = k_ref[rows, :].astype(F32)
            g_rows = _log_sigmoid(gs_ref[rows, :]) * (1.0 / GLA_GATE_NORMALIZER)
        gs_ref[rows, :] = g_rows
        bs_ref[rows, :] = _sel_dot(stack_ref[...], g_rows)

    nc = tb // chunk
    shift = (chunk - 1) if reverse else 1
    row = lax.broadcasted_iota(jnp.int32, (chunk, LANES), 0)
    rr = lax.broadcasted_iota(jnp.int32, (chunk, chunk), 0)
    cc = lax.broadcasted_iota(jnp.int32, (chunk, chunk), 1)
    diff = (cc - rr) if reverse else (rr - cc)
    sub_edge = (LIN_SUB - 1) if reverse else 0
    keep = (row & (LIN_SUB - 1)) != sub_edge
    end_row = 0 if reverse else chunk - 1

    def body(ci, carry):
        c = (nc - 1 - ci) if reverse else ci
        r0 = pl.multiple_of(c * chunk, chunk)
        heads = range(4)
        lanes = [slice(hh * LANES, (hh + 1) * LANES) for hh in heads]
        q = [qs_ref[pl.ds(r0, chunk), ln] for ln in lanes]
        k = [ks_ref[pl.ds(r0, chunk), ln] for ln in lanes]
        g = [gs_ref[pl.ds(r0, chunk), ln] for ln in lanes]
        v = [v_ref[pl.ds(r0, chunk), ln] for ln in lanes]
        b = [bs_ref[pl.ds(r0, chunk), ln] for ln in lanes]

        o = []
        for hh in heads:
            st = st_ref[hh]
            total = b[hh][end_row:end_row + 1, :]
            o.append(_dot_nt((q[hh] * jnp.exp(b[hh])).astype(BF16), st.astype(BF16)))
            k_end = (k[hh] * jnp.exp(total - b[hh])).astype(BF16)
            st_ref[hh] = jnp.exp(total) * st + _dot_tn(v[hh], k_end)

        scores = [jnp.zeros((chunk, chunk), F32) for _ in heads]
        for li, (h, q_half) in enumerate(levels):
            is_q = ((row // h) & 1) == q_half
            ref_row = h - 1 if reverse else h
            for hh in heads:
                b3 = b[hh].reshape(chunk // (2 * h), 2 * h, LANES)
                d3 = b3 - b3[:, ref_row:ref_row + 1, :]
                log_e = -jnp.abs(d3.reshape(chunk, LANES))
                x = (jnp.where(is_q, q[hh], k[hh]) * jnp.exp(log_e)).astype(BF16)
                scores[hh] = scores[hh] + _dot_nt(x, x) * lmask_ref[li]

        for hh in heads:
            f0 = jnp.where(keep, jnp.exp(g[hh]), 0.0)
            a = k[hh]
            band = jnp.where(diff == 0, jnp.sum(q[hh] * a, axis=-1, keepdims=True), 0.0)
            for d in range(1, LIN_SUB):
                a = f0 * pltpu.roll(a, shift, 0)
                band = jnp.where(diff == d, jnp.sum(q[hh] * a, axis=-1, keepdims=True), band)
            scores[hh] = scores[hh] + band

        for hh in heads:
            out = o[hh] + _dot(scores[hh].astype(BF16), v[hh])
            if final:
                os_ref[pl.ds(r0, chunk), lanes[hh]] = out
            else:
                o_ref[pl.ds(r0, chunk), lanes[hh]] = out
        return carry

    lax.fori_loop(0, nc, body, 0, unroll=4)

    if final:
        o = os_ref[...] + of_ref[...]
        if mode == "hgrn":
            ms = jnp.mean(o * o, axis=-1, keepdims=True)
            y = o * lax.rsqrt(ms + 1e-6) * gain_ref[...]
        else:
            parts = []
            for hh in range(4):
                oh = o[:, hh * LANES:(hh + 1) * LANES]
                ms = jnp.mean(oh * oh, axis=-1, keepdims=True)
                parts.append(oh * lax.rsqrt(ms + 1e-6))
            y = jnp.concatenate(parts, axis=1) * gain_ref[...]
        o_ref[...] = (y * _silu(z_ref[...].astype(F32))).astype(o_ref.dtype)


def _gated_scan(h, mode, reverse, consts, params, o_fwd=None):
    bsz, seq, _ = h.shape
    tb = min(SCAN_TOKENS, seq)
    nb = seq // tb
    chunk = LIN_CHUNK
    stack, lmask, levels = consts
    final = reverse

    def blk(n):
        return (nb - 1 - n) if reverse else n

    def col(group):
        return pl.BlockSpec((None, tb, GW), lambda b, n: (b, blk(n), group))

    def full(arr):
        nd = arr.ndim
        return pl.BlockSpec(arr.shape, lambda b, n: (0,) * nd)

    if mode == "hgrn":
        lower_bound, gain = params
        ins = [h, h, h, lower_bound]
        specs = [col(G_HQ), col(G_HFB if reverse else G_HFF), col(G_HI), full(lower_bound)]
        z_group = G_HZ
    else:
        w2, gb, gain = params
        ins = [h, h, h, h, w2, gb]
        specs = [col(G_GQ), col(G_GK), col(G_GV),
                 pl.BlockSpec((None, tb, LANES), lambda b, n: (b, blk(n), NARROW_LOW)),
                 full(w2), full(gb)]
        z_group = G_GZ
    if final:
        ins += [h, o_fwd, gain]
        specs += [col(z_group), pl.BlockSpec((None, tb, GW), lambda b, n: (b, blk(n), 0)), full(gain)]
    ins += [stack, lmask]
    specs += [full(stack), full(lmask)]
    scratch = [pltpu.VMEM((4, LANES, LANES), F32)] + [pltpu.VMEM((tb, GW), F32)] * 4
    if final:
        scratch.append(pltpu.VMEM((tb, GW), F32))
    kernel = functools.partial(_scan_kernel, mode=mode, reverse=reverse, final=final,
                               chunk=chunk, tb=tb, levels=levels, pre_rows=stack.shape[0])
    return pl.pallas_call(
        kernel,
        grid=(bsz, nb),
        in_specs=specs,
        out_specs=pl.BlockSpec((None, tb, GW), lambda b, n: (b, blk(n), 0)),
        out_shape=jax.ShapeDtypeStruct((bsz, seq, GW), BF16 if final else F32),
        scratch_shapes=scratch,
        compiler_params=_params(("arbitrary", "arbitrary")),
        name=f"{mode}_{'bwd' if reverse else 'fwd'}",
    )(*ins)


def _ssd_kernel(*refs, reverse, chunk, tb, nb):
    it = iter(refs)
    if not reverse:
        x_ref, xp_ref, xn_ref, bc_ref, bcp_ref, bcn_ref, cw_ref, cb_ref = (next(it) for _ in range(8))
    else:
        ux_ref, ubc_ref = next(it), next(it)
    dt_ref, dtb_ref, acoef_ref, tri_ref, expand_ref = (next(it) for _ in range(5))
    if reverse:
        z_ref, yf_ref, dskip_ref, gain_ref = (next(it) for _ in range(4))
        o_ref = next(it)
    else:
        o_ref, ux_ref, ubc_ref = next(it), next(it), next(it)
    st_ref, dts_ref, as_ref = next(it), next(it), next(it)
    if reverse:
        ys_ref = next(it)
    else:
        ext_ref = next(it)

    n = pl.program_id(1)
    pos = (nb - 1 - n) if reverse else n

    @pl.when(n == 0)
    def _reset():
        st_ref[...] = jnp.zeros_like(st_ref)

    if not reverse:
        halves = ((x_ref, xp_ref, xn_ref, ux_ref), (bc_ref, bcp_ref, bcn_ref, ubc_ref))
        for hi, (cur, prev, nxt, dst) in enumerate(halves):
            cols = slice(hi * GW, (hi + 1) * GW)
            ext_ref[0:HALO, :] = jnp.where(pos > 0, prev[...].astype(F32), 0.0)
            ext_ref[HALO:HALO + tb, :] = cur[...].astype(F32)
            ext_ref[HALO + tb:2 * HALO + tb, :] = jnp.where(pos < nb - 1, nxt[...].astype(F32), 0.0)
            acc = jnp.broadcast_to(cb_ref[:, cols], (tb, GW))
            for tap in range(SSD_CONV_WIDTH):
                acc = acc + cw_ref[tap:tap + 1, cols] * ext_ref[HALO - 2 + tap:HALO - 2 + tap + tb, :]
            dst[...] = _silu(acc).astype(dst.dtype)

    dt = _softplus(dt_ref[...].astype(F32) + dtb_ref[...])
    dts_ref[...] = dt
    as_ref[...] = dt * acoef_ref[...]

    nc = tb // chunk
    head0 = SSD_HEADS if reverse else 0
    rr = lax.broadcasted_iota(jnp.int32, (chunk, chunk), 0)
    cc = lax.broadcasted_iota(jnp.int32, (chunk, chunk), 1)
    valid = (rr <= cc) if reverse else (rr >= cc)
    lane = lax.broadcasted_iota(jnp.int32, (chunk, LANES), 1)
    low = lane < SSD_HEAD_DIM
    end_row = 0 if reverse else chunk - 1

    def body(ci, carry):
        c = (nc - 1 - ci) if reverse else ci
        r0 = pl.multiple_of(c * chunk, chunk)
        rows = pl.ds(r0, chunk)
        dtc = dts_ref[rows, :]
        acs = _sel_dot(tri_ref[...], as_ref[rows, :])
        a_end = acs[end_row:end_row + 1, :]
        acs_t = jnp.transpose(acs)
        narrow = jnp.concatenate([dtc, jnp.exp(acs), jnp.exp(a_end - acs)], axis=0)
        hi = narrow.astype(BF16)
        lo = (narrow - hi.astype(F32)).astype(BF16)
        wide = _dot(hi, expand_ref[...]) + _dot(lo, expand_ref[...])
        dt_x = wide[0:chunk]
        e_acs = wide[chunk:2 * chunk]
        e_rest = wide[2 * chunk:3 * chunk]

        shared = []
        for g in range(SSD_GROUPS):
            bg = ubc_ref[rows, g * LANES:(g + 1) * LANES]
            cg = ubc_ref[rows, (SSD_GROUPS + g) * LANES:(SSD_GROUPS + g + 1) * LANES]
            st = st_ref[g]
            shared.append((bg, st, _dot_nt(cg, bg), _dot(cg, st.astype(BF16))))
        for g in range(SSD_GROUPS):
            bg, st, gm, c_st = shared[g]
            wx, e_end = [], []
            for jp in range(2):
                pair = 2 * g + jp
                pcols = slice(pair * LANES, (pair + 1) * LANES)
                ea = e_acs[:, pcols]
                xdt = ux_ref[rows, pcols] * dt_x[:, pcols]
                wx.append((e_rest[:, pcols] * xdt).astype(BF16))
                e_end.append(ea[end_row:end_row + 1, :])
                acc = c_st[:, jp * LANES:(jp + 1) * LANES] * ea
                for e in range(2):
                    hl = head0 + 2 * pair + e
                    dmat = acs[:, hl:hl + 1] - acs_t[hl:hl + 1, :]
                    lm = jnp.where(valid, jnp.exp(jnp.minimum(dmat, 0.0)), 0.0)
                    m = (gm * lm).astype(BF16)
                    xm = jnp.where(low if e == 0 else jnp.logical_not(low), xdt, 0.0).astype(BF16)
                    acc = acc + _dot(m, xm)
                if reverse:
                    ys_ref[rows, pcols] = acc
                else:
                    o_ref[rows, pcols] = acc
            st_ref[g] = (jnp.concatenate(e_end, axis=1) * st
                         + _dot_tn(bg, jnp.concatenate(wx, axis=1)))
        return carry

    lax.fori_loop(0, nc, body, 0, unroll=4)

    if reverse:
        y = ys_ref[...] + yf_ref[...] + dskip_ref[...] * ux_ref[...]
        y = y * _silu(z_ref[...].astype(F32))
        ms = jnp.mean(y * y, axis=-1, keepdims=True)
        o_ref[...] = (y * lax.rsqrt(ms + 1e-6) * gain_ref[...]).astype(o_ref.dtype)


def _ssd(h, reverse, params, fwd=None):
    bsz, seq, _ = h.shape
    tb = min(SCAN_TOKENS, seq)
    nb = seq // tb
    chunk = min(SSD_CHUNK, tb)
    conv_w, conv_b, dt_bias, a_coef, tri, expand, d_skip, gain = params
    hb = tb // HALO
    nhb = seq // HALO

    def blk(n):
        return (nb - 1 - n) if reverse else n

    def col(group, width=GW, scale=1):
        return pl.BlockSpec((None, tb, width), lambda b, n: (b, blk(n), group * scale))

    def prev(group):
        return pl.BlockSpec((None, HALO, GW), lambda b, n: (b, jnp.maximum(blk(n) * hb - 1, 0), group))

    def nxt(group):
        return pl.BlockSpec((None, HALO, GW), lambda b, n: (b, jnp.minimum((blk(n) + 1) * hb, nhb - 1), group))

    def full(arr):
        nd = arr.ndim
        return pl.BlockSpec(arr.shape, lambda b, n: (0,) * nd)

    plain = pl.BlockSpec((None, tb, GW), lambda b, n: (b, blk(n), 0))
    dt_spec = pl.BlockSpec((None, tb, LANES), lambda b, n: (b, blk(n), NARROW_DT))
    state = pltpu.VMEM((SSD_GROUPS, SSD_STATE, 2 * LANES), F32)
    narrow = pltpu.VMEM((tb, LANES), F32)
    kernel = functools.partial(_ssd_kernel, reverse=reverse, chunk=chunk, tb=tb, nb=nb)
    if not reverse:
        return pl.pallas_call(
            kernel,
            grid=(bsz, nb),
            in_specs=[col(G_SX), prev(G_SX), nxt(G_SX), col(G_SBC), prev(G_SBC), nxt(G_SBC),
                      full(conv_w), full(conv_b), dt_spec, full(dt_bias), full(a_coef), full(tri),
                      full(expand)],
            out_specs=[plain, plain, plain],
            out_shape=[jax.ShapeDtypeStruct((bsz, seq, GW), F32),
                       jax.ShapeDtypeStruct((bsz, seq, GW), F32),
                       jax.ShapeDtypeStruct((bsz, seq, GW), BF16)],
            scratch_shapes=[state, narrow, narrow, pltpu.VMEM((tb + 2 * HALO, GW), F32)],
            compiler_params=_params(("arbitrary", "arbitrary")),
            name="ssd_fwd",
        )(h, h, h, h, h, h, conv_w, conv_b, h, dt_bias, a_coef, tri, expand)
    y_fwd, u_x, u_bc = fwd
    return pl.pallas_call(
        kernel,
        grid=(bsz, nb),
        in_specs=[plain, plain, dt_spec, full(dt_bias), full(a_coef), full(tri), full(expand),
                  col(G_SZ), plain, full(d_skip), full(gain)],
        out_specs=plain,
        out_shape=jax.ShapeDtypeStruct((bsz, seq, GW), BF16),
        scratch_shapes=[state, narrow, narrow, pltpu.VMEM((tb, GW), F32)],
        compiler_params=_params(("arbitrary", "arbitrary")),
        name="ssd_bwd",
    )(u_x, u_bc, h, dt_bias, a_coef, tri, expand, h, y_fwd, d_skip, gain)


def _out_kernel(ya_ref, yh_ref, ys_ref, yg_ref, x_ref, w_ref, g_ref, b_ref, xo_ref, xb_ref):
    acc = _dot(ya_ref[...], w_ref[0:GW, :])
    acc = acc + _dot(yh_ref[...], w_ref[GW:2 * GW, :])
    acc = acc + _dot(ys_ref[...], w_ref[2 * GW:3 * GW, :])
    acc = acc + _dot(yg_ref[...], w_ref[3 * GW:4 * GW, :])
    r = DEEPNORM_ALPHA * x_ref[...] + acc
    mu = jnp.mean(r, axis=-1, keepdims=True)
    rc = r - mu
    var = jnp.mean(rc * rc, axis=-1, keepdims=True)
    y = rc * lax.rsqrt(var + 1e-5) * g_ref[...] + b_ref[...]
    xo_ref[...] = y
    xb_ref[...] = y.astype(BF16)


def _out_proj_ln(ya, yh, ys, yg, x, w, ln_g, ln_b):
    t, d = x.shape
    tm = min(512, t)
    row = lambda width: pl.BlockSpec((tm, width), lambda i: (i, 0))
    const = lambda arr: pl.BlockSpec(arr.shape, lambda i: (0, 0))
    return pl.pallas_call(
        _out_kernel,
        grid=(t // tm,),
        in_specs=[row(GW), row(GW), row(GW), row(GW), row(d), const(w), const(ln_g), const(ln_b)],
        out_specs=[row(d), row(d)],
        out_shape=[jax.ShapeDtypeStruct((t, d), F32), jax.ShapeDtypeStruct((t, d), BF16)],
        compiler_params=_params(("parallel",)),
        name="out_proj_ln",
    )(ya, yh, ys, yg, x, w, ln_g, ln_b)


def _pad_heads(w, heads, width):
    d = w.shape[0]
    w = w.reshape(d, heads, width)
    w = jnp.pad(w, ((0, 0), (0, 0), (0, LANES - width)))
    return w.reshape(d, heads * LANES)


def _layout_w_in(w):
    d = w.shape[0]
    sizes = (512, 128, 128, 512, 512, 512, 512, 512, 512, 1024, 8, 8, 512, 256, 256, 512, 16, 16, 512)
    offs = np.concatenate([[0], np.cumsum(sizes)])
    (a_q, a_k, a_v, a_z, h_q, h_ff, h_fb, h_i, h_z,
     s_xbc, s_dtf, s_dtb, s_z, g_q, g_k, g_v, g_lf, g_lb, g_z) = [
        w[:, int(offs[i]):int(offs[i + 1])] for i in range(len(sizes))]
    zeros = lambda n: jnp.zeros((d, n), w.dtype)
    groups = [
        a_q, a_z, jnp.concatenate([a_k, a_v, s_dtf, s_dtb, zeros(112), g_lf, g_lb, zeros(96)], axis=1),
        h_q, h_ff, h_fb, h_i, h_z,
        s_xbc[:, :512], s_xbc[:, 512:], s_z,
        _pad_heads(g_q, GLA_HEADS, GLA_HEAD_K), _pad_heads(g_k, GLA_HEADS, GLA_HEAD_K), g_v, g_z,
    ]
    return jnp.concatenate(groups, axis=1).astype(BF16)


def _rope_tables(seq):
    rows = seq // GRID_W
    row_pos = jnp.repeat(jnp.arange(rows, dtype=F32), GRID_W)
    col_pos = jnp.tile(jnp.arange(GRID_W, dtype=F32), rows)
    axis_dim = ATTN_HEAD_DIM // 2
    inv_freq = jnp.power(ROPE_THETA, -jnp.arange(0, axis_dim, 2, dtype=F32) / axis_dim)
    ang_r = row_pos[:, None] * inv_freq
    ang_c = col_pos[:, None] * inv_freq
    cos = jnp.concatenate([jnp.cos(ang_r)] * 2 + [jnp.cos(ang_c)] * 2, axis=1)
    sin = jnp.concatenate([-jnp.sin(ang_r), jnp.sin(ang_r), -jnp.sin(ang_c), jnp.sin(ang_c)], axis=1)
    return jnp.tile(cos, (1, 2)), jnp.tile(sin, (1, 2))


def _ssd_expand(reverse):
    m = np.zeros((LANES, GW), np.float32)
    for hd in range(SSD_HEADS):
        m[hd + (SSD_HEADS if reverse else 0), hd * SSD_HEAD_DIM:(hd + 1) * SSD_HEAD_DIM] = 1.0
    return jnp.asarray(m, BF16)


def _tri(chunk, reverse):
    t = np.arange(chunk)[:, None]
    u = np.arange(chunk)[None, :]
    return jnp.asarray((u >= t) if reverse else (u <= t), BF16)


def kernel(x, w_in, attn_q_norm, attn_k_norm, hgrn_lb_logits, hgrn_norm, ssd_conv_w, ssd_conv_b,
           ssd_dt_bias, ssd_a_log, ssd_d, ssd_norm, gla_gk_w2, gla_gk_b, gla_norm, w_out, ln_g, ln_b):
    bsz, seq, d = x.shape
    t = bsz * seq
    rope = _rope_tables(seq)
    blocksum = jnp.asarray(np.kron(np.eye(GW // ATTN_HEAD_DIM), np.ones((ATTN_HEAD_DIM, ATTN_HEAD_DIM))), BF16)
    lower_bounds = jnp.cumsum(jax.nn.softmax(hgrn_lb_logits.astype(F32), axis=0), axis=0)
    lower_bounds = lower_bounds - lower_bounds[0]
    scan_consts = {rev: _scan_constants(LIN_CHUNK, min(SCAN_PRE_ROWS, seq), rev) for rev in (False, True)}
    ssd_chunk = min(SSD_CHUNK, min(SCAN_TOKENS, seq))

    xf = x.reshape(t, d)
    xb = xf
    for i in range(DEPTH):
        h = _in_proj(xb, _layout_w_in(w_in[i])).reshape(bsz, seq, N_PROJ)

        y_a = _attention(h, rope, jnp.tile(attn_q_norm[i], ATTN_HEADS)[None, :].astype(F32),
                         jnp.tile(attn_k_norm[i], 2)[None, :].astype(F32), blocksum)

        lb = jnp.maximum(lower_bounds[i], 0.0)[None, :]
        hgrn_params = (lb, hgrn_norm[i][None, :].astype(F32))
        o_f = _gated_scan(h, "hgrn", False, scan_consts[False], hgrn_params)
        y_h = _gated_scan(h, "hgrn", True, scan_consts[True], hgrn_params, o_f)

        gla_out = None
        for rev in (False, True):
            w2 = jnp.zeros((LANES, GW), F32)
            w2 = w2.at[rev * GLA_GATE_RANK:(rev + 1) * GLA_GATE_RANK].set(
                _pad_heads(gla_gk_w2[i, int(rev)], GLA_HEADS, GLA_HEAD_K))
            gb = _pad_heads(gla_gk_b[i, int(rev)][None, :], GLA_HEADS, GLA_HEAD_K)
            gla_params = (w2.astype(BF16), gb.astype(F32), jnp.tile(gla_norm[i], GLA_HEADS)[None, :].astype(F32))
            gla_out = _gated_scan(h, "gla", rev, scan_consts[rev], gla_params, gla_out)
        y_g = gla_out

        dt_bias = jnp.zeros((1, LANES), F32).at[0, :2 * SSD_HEADS].set(ssd_dt_bias[i].astype(F32).reshape(-1))
        a_coef = jnp.zeros((1, LANES), F32).at[0, :2 * SSD_HEADS].set(-jnp.exp(ssd_a_log[i].astype(F32)).reshape(-1))
        ssd_out = None
        for rev in (False, True):
            ssd_params = (ssd_conv_w[i].astype(F32), ssd_conv_b[i][None, :].astype(F32), dt_bias, a_coef,
                          _tri(ssd_chunk, rev), _ssd_expand(rev), jnp.repeat(ssd_d[i].astype(F32), SSD_HEAD_DIM)[None, :],
                          ssd_norm[i][None, :].astype(F32))
            ssd_out = _ssd(h, rev, ssd_params, ssd_out)
        y_s = ssd_out

        flat = lambda a: a.reshape(t, GW)
        xf, xb = _out_proj_ln(flat(y_a), flat(y_h), flat(y_s), flat(y_g), xf, w_out[i].astype(BF16),
                              ln_g[i][None, :].astype(F32), ln_b[i][None, :].astype(F32))
    return xf.reshape(bsz, seq, d)
```
